```python
import math, functools
import jax, jax.numpy as jnp
from jax import lax
import numpy as np

D_MODEL = 1024
BATCH = 2
SEQ = 8192
DEPTH = 1
DEC_BATCH = 128
DEC_SEQ = 8
PAST_LEN = 2048
PAGE_SIZE = 128

SSM_WIDTH = D_MODEL // 2
SSM_GROUP = 16
SSM_GROUPS = SSM_WIDTH // SSM_GROUP
SSM_STATE = 64
N_HEADS = 8
HEAD_DIM = 64
V_DIM = 2 * HEAD_DIM
ATT_WIDTH = N_HEADS * V_DIM
Q_BLOCK = 128
N_EXPERT_GROUPS = 4
EXPERTS_PER_GROUP = 8
N_EXPERTS = N_EXPERT_GROUPS * EXPERTS_PER_GROUP
EXPERT_FF = D_MODEL // 4
TOP_K = 2
MOE_BLOCK = 128
DN_ALPHA = (2.0 * DEPTH) ** 0.25
DN_BETA = (8.0 * DEPTH) ** -0.25
LN_EPS = 1e-5
NEG_INF = -1e30
C_U = 0
C_Q = C_U + SSM_WIDTH
C_K = C_Q + N_HEADS * 2 * HEAD_DIM
C_V = C_K + N_HEADS * 2 * HEAD_DIM
C_G = C_V + ATT_WIDTH
C_END = C_G + 2 * D_MODEL

kernel_name = "hybrid_s5_diffattn_hmoe_step"

F32 = jnp.float32


def layer_norm(x, g, b):
    xf = x.astype(F32)
    mu = jnp.mean(xf, -1, keepdims=True)
    var = jnp.mean(jnp.square(xf - mu), -1, keepdims=True)
    return ((xf - mu) * lax.rsqrt(var + LN_EPS) * g.astype(F32) + b.astype(F32)).astype(x.dtype)


def cmul(ar, ai, br, bi):
    return ar * br - ai * bi, ar * bi + ai * br


def lambda_init_of(layer_idx):
    return 0.8 - 0.6 * math.exp(-0.3 * layer_idx)


def alibi_slopes():
    return jnp.exp2(-8.0 * (jnp.arange(N_HEADS, dtype=F32) + 1.0) / N_HEADS)


def ssm_discretize(lam_re, lam_im, log_dt, b_re, b_im):
    dt = jnp.exp(log_dt.astype(F32))[:, None]
    lr, li = lam_re.astype(F32), lam_im.astype(F32)
    mag = jnp.exp(lr * dt)
    a_re, a_im = mag * jnp.cos(li * dt), mag * jnp.sin(li * dt)
    den = lr * lr + li * li
    zr, zi = a_re - 1.0, a_im
    fr = (zr * lr + zi * li) / den
    fi = (zi * lr - zr * li) / den
    bb_re, bb_im = cmul(fr[..., None], fi[..., None], b_re.astype(F32), b_im.astype(F32))
    return a_re, a_im, bb_re, bb_im


def ssm_combine(e1, e2):
    a1r, a1i, b1r, b1i = e1
    a2r, a2i, b2r, b2i = e2
    ar, ai = cmul(a2r, a2i, a1r, a1i)
    br, bi = cmul(a2r, a2i, b1r, b1i)
    return ar, ai, br + b2r, bi + b2i


def ssm_branch(u, s0_re, s0_im, p):
    bsz, L, _ = u.shape
    uf = u.astype(F32).reshape(bsz, L, SSM_GROUPS, SSM_GROUP)
    a_re, a_im, bb_re, bb_im = ssm_discretize(p["ssm_lambda_re"], p["ssm_lambda_im"], p["ssm_log_dt"],
                                              p["ssm_b_re"], p["ssm_b_im"])
    bu_re = jnp.einsum('gpc,blgc->blgp', bb_re, uf)
    bu_im = jnp.einsum('gpc,blgc->blgp', bb_im, uf)
    i_re, i_im = cmul(a_re, a_im, s0_re.astype(F32), s0_im.astype(F32))
    bu_re = bu_re.at[:, 0].add(i_re)
    bu_im = bu_im.at[:, 0].add(i_im)
    ar = jnp.broadcast_to(a_re, bu_re.shape)
    ai = jnp.broadcast_to(a_im, bu_im.shape)
    _, _, s_re, s_im = lax.associative_scan(ssm_combine, (ar, ai, bu_re, bu_im), axis=1)
    c_re, c_im = p["ssm_c_re"].astype(F32), p["ssm_c_im"].astype(F32)
    y = (jnp.einsum('gcp,blgp->blgc', c_re, s_re) - jnp.einsum('gcp,blgp->blgc', c_im, s_im)
         + p["ssm_d"].astype(F32) * uf)
    y = jax.nn.gelu(y.reshape(bsz, L, SSM_WIDTH)).astype(u.dtype)
    y = y * jax.nn.sigmoid(y @ p["w_glu"] + p["b_glu"])
    return y, s_re[:, -1], s_im[:, -1]


def diff_lambda(p, lam_init):
    lq1, lk1 = p["lambda_q1"].astype(F32), p["lambda_k1"].astype(F32)
    lq2, lk2 = p["lambda_q2"].astype(F32), p["lambda_k2"].astype(F32)
    return jnp.exp(jnp.sum(lq1 * lk1)) - jnp.exp(jnp.sum(lq2 * lk2)) + lam_init


def diff_attn_core(q, k, v, q_pos, k_pos, lam):
    s = jnp.einsum('bqhmd,bkhmd->bhmqk', q, k).astype(F32) * (HEAD_DIM ** -0.5)
    dist = (q_pos[:, None] - k_pos[None, :]).astype(F32)
    s = s - alibi_slopes()[:, None, None, None] * dist
    s = jnp.where(dist >= 0, s, NEG_INF)
    prob = jax.nn.softmax(s, axis=-1)
    w = prob[:, :, 0] - lam * prob[:, :, 1]
    return jnp.einsum('bhqk,bkhd->bqhd', w.astype(v.dtype), v)


def diff_attn_prompt(q, k, v, lam):
    bsz, L = q.shape[0], q.shape[1]
    k_pos = jnp.arange(L)

    def one_block(i):
        start = i * Q_BLOCK
        qb = lax.dynamic_slice_in_dim(q, start, Q_BLOCK, axis=1)
        return diff_attn_core(qb, k, v, start + jnp.arange(Q_BLOCK), k_pos, lam)

    o = lax.map(one_block, jnp.arange(L // Q_BLOCK))
    return jnp.moveaxis(o, 0, 1).reshape(bsz, L, N_HEADS, V_DIM)


def diff_attn_sample(q, k, v, lam, past_k, past_v):
    n_past, t_new = past_k.shape[1], q.shape[1]
    k_all = jnp.concatenate([past_k.astype(k.dtype), k], axis=1)
    v_all = jnp.concatenate([past_v.astype(v.dtype), v], axis=1)
    q_pos = n_past + jnp.arange(t_new)
    k_pos = jnp.arange(n_past + t_new)
    return diff_attn_core(q, k_all, v_all, q_pos, k_pos, lam)


def head_rms(o, g, lam_init):
    of = o.astype(F32)
    of = of * lax.rsqrt(jnp.mean(of * of, -1, keepdims=True) + LN_EPS) * g.astype(F32)
    return (of * (1.0 - lam_init)).astype(o.dtype)


def hier_moe(x, p):
    t = x.shape[0]
    g_logits = (x @ p["w_router_group"] + p["b_router_group"]).astype(F32)
    g_prob = jax.nn.softmax(g_logits, -1)
    _, g_sel = lax.top_k(g_logits, 1)
    p_g = jnp.take_along_axis(g_prob, g_sel, -1)
    e_logits = (x @ p["w_router_expert"] + p["b_router_expert"]).astype(F32)
    e_logits = e_logits.reshape(t, N_EXPERT_GROUPS, EXPERTS_PER_GROUP)
    e_in = jnp.take_along_axis(e_logits, g_sel[:, :, None], 1)[:, 0]
    top_v, top_i = lax.top_k(e_in, TOP_K)
    w2 = jax.nn.softmax(top_v, -1) * p_g
    e_idx = g_sel * EXPERTS_PER_GROUP + top_i
    combine = jnp.sum(jax.nn.one_hot(e_idx, N_EXPERTS, dtype=F32) * w2[..., None], axis=1)
    h = (jax.nn.silu(jnp.einsum('td,edf->tef', x, p["w_exp_gate"]))
         * jnp.einsum('td,edf->tef', x, p["w_exp_up"]))
    h = h * combine[..., None].astype(h.dtype)
    return jnp.einsum('tef,efd->td', h, p["w_exp_down"])


def layer_forward(x, s0_re, s0_im, attn_fn, p, lam_init, moe_in_blocks):
    bsz, L, _ = x.shape
    proj = x @ p["w_in"]
    u = proj[..., C_U:C_Q]
    q = proj[..., C_Q:C_K].reshape(bsz, L, N_HEADS, 2, HEAD_DIM)
    k = proj[..., C_K:C_V].reshape(bsz, L, N_HEADS, 2, HEAD_DIM)
    v = proj[..., C_V:C_G].reshape(bsz, L, N_HEADS, V_DIM)
    gates = jax.nn.sigmoid(proj[..., C_G:C_END] + p["b_gate"])
    ssm_y, s_re, s_im = ssm_branch(u, s0_re, s0_im, p)
    lam = diff_lambda(p, lam_init)
    att = head_rms(attn_fn(q, k, v, lam), p["subln_g"], lam_init).reshape(bsz, L, ATT_WIDTH)
    merged = (gates[..., :D_MODEL] * (ssm_y @ p["p_ssm"])
              + gates[..., D_MODEL:] * (att.astype(x.dtype) @ p["p_att"]))
    x1 = layer_norm(DN_ALPHA * x + merged @ p["w_o"], p["ln1_g"], p["ln1_b"])
    if moe_in_blocks:
        xb = x1.reshape(-1, MOE_BLOCK, D_MODEL)
        ffn = lax.map(lambda xt: hier_moe(xt, p), xb).reshape(x1.shape)
    else:
        ffn = hier_moe(x1.reshape(-1, D_MODEL), p).reshape(x1.shape)
    y = layer_norm(DN_ALPHA * x1 + ffn, p["ln2_g"], p["ln2_b"])
    return y, k.reshape(bsz, L, N_HEADS, 2 * HEAD_DIM), v, s_re, s_im


def setup_inputs(seed: int = 0) -> dict:
    key = jax.random.key(seed)
    ks = jax.random.split(key, 40)
    n_pages = PAST_LEN // PAGE_SIZE
    n_used = DEC_BATCH * n_pages
    n_phys = n_used + n_used // 4

    def nrm(i, shape, scale):
        return jax.random.normal(ks[i], shape, F32) * scale

    x_prompt = nrm(0, (BATCH, SEQ, D_MODEL), 1.0)
    x_sample = nrm(1, (DEC_BATCH, DEC_SEQ, D_MODEL), 1.0)
    cache_k = nrm(2, (DEPTH, n_phys, PAGE_SIZE, N_HEADS, 2 * HEAD_DIM), 1.0)
    cache_v = nrm(3, (DEPTH, n_phys, PAGE_SIZE, N_HEADS, V_DIM), DN_BETA)
    state_ssm_re = nrm(4, (DEPTH, DEC_BATCH, SSM_GROUPS, SSM_STATE), 0.5)
    state_ssm_im = nrm(5, (DEPTH, DEC_BATCH, SSM_GROUPS, SSM_STATE), 0.5)
    page_table = jax.random.permutation(ks[6], n_phys)[:n_used].reshape(DEC_BATCH, n_pages).astype(jnp.int32)

    col_scale = jnp.ones((C_END,), F32).at[C_V:C_G].set(DN_BETA)
    w_in = nrm(7, (DEPTH, D_MODEL, C_END), D_MODEL ** -0.5) * col_scale
    b_gate = nrm(8, (DEPTH, 2 * D_MODEL), 0.01)
    ssm_lambda_re = -0.5 + nrm(9, (DEPTH, SSM_GROUPS, SSM_STATE), 0.01)
    ssm_lambda_im = (jnp.pi * jnp.arange(SSM_STATE, dtype=F32)
                     + nrm(10, (DEPTH, SSM_GROUPS, SSM_STATE), 0.01))
    ssm_log_dt = jax.random.uniform(ks[11], (DEPTH, SSM_GROUPS), F32, math.log(1e-3), math.log(1e-1))
    ssm_b_re = nrm(12, (DEPTH, SSM_GROUPS, SSM_STATE, SSM_GROUP), (2 * SSM_GROUP) ** -0.5)
    ssm_b_im = nrm(13, (DEPTH, SSM_GROUPS, SSM_STATE, SSM_GROUP), (2 * SSM_GROUP) ** -0.5)
    ssm_c_re = nrm(14, (DEPTH, SSM_GROUPS, SSM_GROUP, SSM_STATE), (2 * SSM_STATE) ** -0.5)
    ssm_c_im = nrm(15, (DEPTH, SSM_GROUPS, SSM_GROUP, SSM_STATE), (2 * SSM_STATE) ** -0.5)
    ssm_d = nrm(16, (DEPTH, SSM_GROUPS, SSM_GROUP), 1.0)
    w_glu = nrm(17, (DEPTH, SSM_WIDTH, SSM_WIDTH), SSM_WIDTH ** -0.5)
    b_glu = nrm(18, (DEPTH, SSM_WIDTH), 0.01)
    lambda_q1 = nrm(19, (DEPTH, HEAD_DIM), 0.1)
    lambda_k1 = nrm(20, (DEPTH, HEAD_DIM), 0.1)
    lambda_q2 = nrm(21, (DEPTH, HEAD_DIM), 0.1)
    lambda_k2 = nrm(22, (DEPTH, HEAD_DIM), 0.1)
    subln_g = 1.0 + nrm(23, (DEPTH, V_DIM), 0.01)
    p_ssm = nrm(24, (DEPTH, SSM_WIDTH, D_MODEL), DN_BETA * SSM_WIDTH ** -0.5)
    p_att = nrm(25, (DEPTH, ATT_WIDTH, D_MODEL), DN_BETA * ATT_WIDTH ** -0.5)
    w_o = nrm(26, (DEPTH, D_MODEL, D_MODEL), DN_BETA * D_MODEL ** -0.5)
    ln1_g = 1.0 + nrm(27, (DEPTH, D_MODEL), 0.01)
    ln1_b = nrm(28, (DEPTH, D_MODEL), 0.01)
    w_router_group = nrm(29, (DEPTH, D_MODEL, N_EXPERT_GROUPS), D_MODEL ** -0.5)
    b_router_group = nrm(30, (DEPTH, N_EXPERT_GROUPS), 0.01)
    w_router_expert = nrm(31, (DEPTH, D_MODEL, N_EXPERTS), D_MODEL ** -0.5)
    b_router_expert = nrm(32, (DEPTH, N_EXPERTS), 0.01)
    w_exp_gate = nrm(33, (DEPTH, N_EXPERTS, D_MODEL, EXPERT_FF), D_MODEL ** -0.5)
    w_exp_up = nrm(34, (DEPTH, N_EXPERTS, D_MODEL, EXPERT_FF), DN_BETA * D_MODEL ** -0.5)
    w_exp_down = nrm(35, (DEPTH, N_EXPERTS, EXPERT_FF, D_MODEL), DN_BETA * EXPERT_FF ** -0.5)
    ln2_g = 1.0 + nrm(36, (DEPTH, D_MODEL), 0.01)
    ln2_b = nrm(37, (DEPTH, D_MODEL), 0.01)
    return {
        "x_prompt": x_prompt, "x_sample": x_sample,
        "cache_k": cache_k, "cache_v": cache_v,
        "state_ssm_re": state_ssm_re, "state_ssm_im": state_ssm_im,
        "page_table": page_table,
        "w_in": w_in, "b_gate": b_gate,
        "ssm_lambda_re": ssm_lambda_re, "ssm_lambda_im": ssm_lambda_im, "ssm_log_dt": ssm_log_dt,
        "ssm_b_re": ssm_b_re, "ssm_b_im": ssm_b_im, "ssm_c_re": ssm_c_re, "ssm_c_im": ssm_c_im,
        "ssm_d": ssm_d, "w_glu": w_glu, "b_glu": b_glu,
        "lambda_q1": lambda_q1, "lambda_k1": lambda_k1, "lambda_q2": lambda_q2, "lambda_k2": lambda_k2,
        "subln_g": subln_g, "p_ssm": p_ssm, "p_att": p_att, "w_o": w_o,
        "ln1_g": ln1_g, "ln1_b": ln1_b,
        "w_router_group": w_router_group, "b_router_group": b_router_group,
        "w_router_expert": w_router_expert, "b_router_expert": b_router_expert,
        "w_exp_gate": w_exp_gate, "w_exp_up": w_exp_up, "w_exp_down": w_exp_down,
        "ln2_g": ln2_g, "ln2_b": ln2_b,
    }


def reference(x_prompt, x_sample, cache_k, cache_v, state_ssm_re, state_ssm_im, page_table,
              w_in, b_gate, ssm_lambda_re, ssm_lambda_im, ssm_log_dt, ssm_b_re, ssm_b_im,
              ssm_c_re, ssm_c_im, ssm_d, w_glu, b_glu, lambda_q1, lambda_k1, lambda_q2, lambda_k2,
              subln_g, p_ssm, p_att, w_o, ln1_g, ln1_b, w_router_group, b_router_group,
              w_router_expert, b_router_expert, w_exp_gate, w_exp_up, w_exp_down, ln2_g, ln2_b):
    bsz = x_prompt.shape[0]
    dec_b = x_sample.shape[0]
    n_past = page_table.shape[1] * PAGE_SIZE
    hp, hs = x_prompt, x_sample
    kp_l, vp_l, srp_l, sip_l, ks_l, vs_l, srs_l, sis_l = [], [], [], [], [], [], [], []
    for l in range(DEPTH):
        p = dict(w_in=w_in[l], b_gate=b_gate[l], ssm_lambda_re=ssm_lambda_re[l], ssm_lambda_im=ssm_lambda_im[l],
                 ssm_log_dt=ssm_log_dt[l], ssm_b_re=ssm_b_re[l], ssm_b_im=ssm_b_im[l], ssm_c_re=ssm_c_re[l],
                 ssm_c_im=ssm_c_im[l], ssm_d=ssm_d[l], w_glu=w_glu[l], b_glu=b_glu[l],
                 lambda_q1=lambda_q1[l], lambda_k1=lambda_k1[l], lambda_q2=lambda_q2[l], lambda_k2=lambda_k2[l],
                 subln_g=subln_g[l], p_ssm=p_ssm[l], p_att=p_att[l], w_o=w_o[l], ln1_g=ln1_g[l], ln1_b=ln1_b[l],
                 w_router_group=w_router_group[l], b_router_group=b_router_group[l],
                 w_router_expert=w_router_expert[l], b_router_expert=b_router_expert[l],
                 w_exp_gate=w_exp_gate[l], w_exp_up=w_exp_up[l], w_exp_down=w_exp_down[l],
                 ln2_g=ln2_g[l], ln2_b=ln2_b[l])
        lam_init = lambda_init_of(l)
        zeros = jnp.zeros((bsz, SSM_GROUPS, SSM_STATE), F32)
        hp, kp, vp, srp, sip = layer_forward(hp, zeros, zeros, diff_attn_prompt, p, lam_init, True)
        past_k = cache_k[l][page_table].reshape(dec_b, n_past, N_HEADS, 2, HEAD_DIM)
        past_v = cache_v[l][page_table].reshape(dec_b, n_past, N_HEADS, V_DIM)
        attn_s = functools.partial(diff_attn_sample, past_k=past_k, past_v=past_v)
        hs, ks, vs, srs, sis = layer_forward(hs, state_ssm_re[l], state_ssm_im[l], attn_s, p, lam_init, False)
        kp_l.append(kp); vp_l.append(vp); srp_l.append(srp); sip_l.append(sip)
        ks_l.append(ks); vs_l.append(vs); srs_l.append(srs); sis_l.append(sis)
    return (hp, hs,
            jnp.stack(kp_l), jnp.stack(vp_l), jnp.stack(srp_l), jnp.stack(sip_l),
            jnp.stack(ks_l), jnp.stack(vs_l), jnp.stack(srs_l), jnp.stack(sis_l))
```

```python
import functools
import math

import jax
import jax.numpy as jnp
from jax import lax
from jax.experimental import pallas as pl
from jax.experimental.pallas import tpu as pltpu

F32 = jnp.float32
BF16 = jnp.bfloat16

D_MODEL = 1024
N_HEADS = 8
HEAD_DIM = 64
V_DIM = 2 * HEAD_DIM
ATT_WIDTH = N_HEADS * V_DIM
SSM_WIDTH = 512
SSM_GROUP = 16
SSM_GROUPS = SSM_WIDTH // SSM_GROUP
SSM_STATE = 64
N_STATE = SSM_GROUPS * SSM_STATE
N_EXPERT_GROUPS = 4
EXPERTS_PER_GROUP = 8
N_EXPERTS = N_EXPERT_GROUPS * EXPERTS_PER_GROUP
EXPERT_FF = 256
PAGE_SIZE = 128
DEPTH = 1
DN_ALPHA = (2.0 * DEPTH) ** 0.25
LN_EPS = 1e-5
NEG_INF = -1e30
QK_SCALE = HEAD_DIM ** -0.5
C_Q = SSM_WIDTH
C_K = C_Q + ATT_WIDTH
C_V = C_K + ATT_WIDTH
C_G = C_V + ATT_WIDTH

VMEM_LIMIT_BYTES = 52 * 1024 * 1024
SUBLANES = 8


def _cparams(sem):
    return pltpu.CompilerParams(dimension_semantics=sem, vmem_limit_bytes=VMEM_LIMIT_BYTES)


def _const_spec(shape):
    nd = len(shape)
    return pl.BlockSpec(shape, lambda *_: (0,) * nd)


def _layer_norm(x, g, b):
    mu = jnp.mean(x, axis=-1, keepdims=True)
    xc = x - mu
    var = jnp.mean(xc * xc, axis=-1, keepdims=True)
    return xc * lax.rsqrt(var + LN_EPS) * g + b


def _lambda_init(layer_idx):
    return 0.8 - 0.6 * math.exp(-0.3 * layer_idx)


def _diff_lambda(lamp, lam_init):
    a = jnp.sum(lamp[0:1, :] * lamp[1:2, :], axis=-1, keepdims=True)
    b = jnp.sum(lamp[2:3, :] * lamp[3:4, :], axis=-1, keepdims=True)
    return jnp.exp(a) - jnp.exp(b) + lam_init


def _proj_kernel(x_ref, w_ref, u_ref, q_ref, kf_ref, kb_ref, vf_ref, vb_ref):
    x = x_ref[...].astype(BF16)
    u_ref[...] = jnp.dot(x, w_ref[:, 0:C_Q], preferred_element_type=F32)
    q = jnp.dot(x, w_ref[:, C_Q:C_K], preferred_element_type=F32)
    q_ref[...] = (q * QK_SCALE).astype(BF16)
    k = jnp.dot(x, w_ref[:, C_K:C_V], preferred_element_type=F32)
    kb_ref[...] = k.astype(BF16)
    v = jnp.dot(x, w_ref[:, C_V:C_G], preferred_element_type=F32)
    vb_ref[...] = v.astype(BF16)
    for h in range(N_HEADS):
        kf_ref[:, h, :] = k[:, h * V_DIM:(h + 1) * V_DIM]
        vf_ref[:, h, :] = v[:, h * V_DIM:(h + 1) * V_DIM]


def _project(x2d, w_bf16, tm):
    t = x2d.shape[0]
    row = lambda n: pl.BlockSpec((tm, n), lambda i: (i, 0))
    heads = pl.BlockSpec((tm, N_HEADS, V_DIM), lambda i: (i, 0, 0))
    return pl.pallas_call(
        _proj_kernel,
        grid=(t // tm,),
        in_specs=[row(D_MODEL), _const_spec((D_MODEL, C_G))],
        out_specs=[row(SSM_WIDTH), row(ATT_WIDTH), heads, row(ATT_WIDTH), heads, row(ATT_WIDTH)],
        out_shape=[
            jax.ShapeDtypeStruct((t, SSM_WIDTH), F32),
            jax.ShapeDtypeStruct((t, ATT_WIDTH), BF16),
            jax.ShapeDtypeStruct((t, N_HEADS, V_DIM), F32),
            jax.ShapeDtypeStruct((t, ATT_WIDTH), BF16),
            jax.ShapeDtypeStruct((t, N_HEADS, V_DIM), F32),
            jax.ShapeDtypeStruct((t, ATT_WIDTH), BF16),
        ],
        compiler_params=_cparams(("parallel",)),
        name="proj",
    )(x2d, w_bf16)


def _ssm_tables(lam_re, lam_im, log_dt, b_re, b_im, c_re, c_im):
    dt = jnp.exp(log_dt)[:, None]
    mag = jnp.exp(lam_re * dt)
    a_re, a_im = mag * jnp.cos(lam_im * dt), mag * jnp.sin(lam_im * dt)
    den = lam_re * lam_re + lam_im * lam_im
    zr, zi = a_re - 1.0, a_im
    fr = (zr * lam_re + zi * lam_im) / den
    fi = (zi * lam_re - zr * lam_im) / den
    bb_re = fr[..., None] * b_re - fi[..., None] * b_im
    bb_im = fr[..., None] * b_im + fi[..., None] * b_re
    eye = jnp.eye(SSM_GROUPS, dtype=F32)
    blk_b = lambda m: jnp.einsum('gpc,gh->gchp', m, eye).reshape(SSM_WIDTH, N_STATE).astype(BF16)
    blk_c = lambda m: jnp.einsum('gcp,gh->gphc', m, eye).reshape(N_STATE, SSM_WIDTH).astype(BF16)
    ar, ai = a_re.reshape(1, N_STATE), a_im.reshape(1, N_STATE)

    def cmul(xr, xi, yr, yi):
        return xr * yr - xi * yi, xr * yi + xi * yr

    rows = lax.broadcasted_iota(jnp.int32, (SUBLANES, 1), 0)
    tabs = []
    pr, pi = ar, ai
    for k in range(3):
        keep = rows >= (1 << k)
        tabs += [jnp.where(keep, pr, 0.0), jnp.where(keep, pi, 0.0)]
        pr, pi = cmul(pr, pi, pr, pi)
    cr, ci = [ar], [ai]
    for _ in range(SUBLANES - 1):
        nr, ni = cmul(cr[-1], ci[-1], ar, ai)
        cr.append(nr)
        ci.append(ni)
    tabs += [jnp.concatenate(cr, axis=0), jnp.concatenate(ci, axis=0)]
    return blk_b(bb_re), blk_b(bb_im), blk_c(c_re), blk_c(-c_im), jnp.stack(tabs)


SCAN_LANES = 512


def _scan_rows(sre, sim, mult_ref, r0, carry_re, carry_im):
    out_re, out_im = [], []
    for c0 in range(0, N_STATE, SCAN_LANES):
        ls = slice(c0, c0 + SCAN_LANES)
        xr = sre[pl.ds(r0, SUBLANES), ls]
        xi = sim[pl.ds(r0, SUBLANES), ls]
        for k in range(3):
            d = 1 << k
            yr = pltpu.roll(xr, d, 0)
            yi = pltpu.roll(xi, d, 0)
            mr = mult_ref[2 * k, :, ls]
            mi = mult_ref[2 * k + 1, :, ls]
            xr, xi = xr + (mr * yr - mi * yi), xi + (mr * yi + mi * yr)
        pr = mult_ref[6, :, ls]
        pi = mult_ref[7, :, ls]
        cr = carry_re[:, ls]
        ci = carry_im[:, ls]
        xr, xi = xr + (pr * cr - pi * ci), xi + (pr * ci + pi * cr)
        sre[pl.ds(r0, SUBLANES), ls] = xr
        sim[pl.ds(r0, SUBLANES), ls] = xi
        out_re.append(xr[SUBLANES - 1:SUBLANES, :])
        out_im.append(xi[SUBLANES - 1:SUBLANES, :])
    return jnp.concatenate(out_re, axis=1), jnp.concatenate(out_im, axis=1)


def _gelu_tanh(x):
    return 0.5 * x * (1.0 + jnp.tanh(math.sqrt(2.0 / math.pi) * (x + 0.044715 * (x * x * x))))


def _ssm_head(u_ref, bre_ref, bim_ref, sre, sim):
    ub = u_ref[...].astype(BF16)
    sre[...] = jnp.dot(ub, bre_ref[...], preferred_element_type=F32)
    sim[...] = jnp.dot(ub, bim_ref[...], preferred_element_type=F32)


def _ssm_tail(u_ref, cre_ref, ncim_ref, d_ref, wglu_ref, bglu_ref, y_ref, sre, sim):
    y = (jnp.dot(sre[...].astype(BF16), cre_ref[...], preferred_element_type=F32)
         + jnp.dot(sim[...].astype(BF16), ncim_ref[...], preferred_element_type=F32)
         + d_ref[...] * u_ref[...])
    y = _gelu_tanh(y)
    z = jnp.dot(y.astype(BF16), wglu_ref[...], preferred_element_type=F32) + bglu_ref[...]
    y_ref[...] = (y * jax.nn.sigmoid(z)).astype(BF16)


def _ssm_seq_kernel(u_ref, bre_ref, bim_ref, cre_ref, ncim_ref, d_ref, wglu_ref, bglu_ref, mult_ref,
                    y_ref, ore_ref, oim_ref, sre, sim, car_re, car_im):
    @pl.when(pl.program_id(1) == 0)
    def _():
        car_re[...] = jnp.zeros_like(car_re)
        car_im[...] = jnp.zeros_like(car_im)

    _ssm_head(u_ref, bre_ref, bim_ref, sre, sim)
    n_blk = u_ref.shape[0] // SUBLANES

    def body(n, carry):
        r0 = pl.multiple_of(n * SUBLANES, SUBLANES)
        return _scan_rows(sre, sim, mult_ref, r0, carry[0], carry[1])

    cr, ci = lax.fori_loop(0, n_blk, body, (car_re[...], car_im[...]))
    car_re[...] = cr
    car_im[...] = ci
    ore_ref[0] = cr
    oim_ref[0] = ci
    _ssm_tail(u_ref, cre_ref, ncim_ref, d_ref, wglu_ref, bglu_ref, y_ref, sre, sim)


def _ssm_dec_kernel(u_ref, s0re_ref, s0im_ref, bre_ref, bim_ref, cre_ref, ncim_ref, d_ref, wglu_ref, bglu_ref,
                    mult_ref, y_ref, ore_ref, oim_ref, sre, sim):
    _ssm_head(u_ref, bre_ref, bim_ref, sre, sim)
    n_blk = u_ref.shape[0] // SUBLANES

    def body(n, _):
        r0 = pl.multiple_of(n * SUBLANES, SUBLANES)
        cr, ci = _scan_rows(sre, sim, mult_ref, r0, s0re_ref[pl.ds(n, 1), :], s0im_ref[pl.ds(n, 1), :])
        ore_ref[pl.ds(n, 1), :] = cr
        oim_ref[pl.ds(n, 1), :] = ci
        return 0

    lax.fori_loop(0, n_blk, body, 0)
    _ssm_tail(u_ref, cre_ref, ncim_ref, d_ref, wglu_ref, bglu_ref, y_ref, sre, sim)


def _ssm_weight_specs():
    return [_const_spec((SSM_WIDTH, N_STATE)), _const_spec((SSM_WIDTH, N_STATE)),
            _const_spec((N_STATE, SSM_WIDTH)), _const_spec((N_STATE, SSM_WIDTH)),
            _const_spec((1, SSM_WIDTH)), _const_spec((SSM_WIDTH, SSM_WIDTH)), _const_spec((1, SSM_WIDTH)),
            _const_spec((8, SUBLANES, N_STATE))]


def _ssm_prompt(u3, wts, lc):
    bsz, seq, _ = u3.shape
    return pl.pallas_call(
        _ssm_seq_kernel,
        grid=(bsz, seq // lc),
        in_specs=[pl.BlockSpec((None, lc, SSM_WIDTH), lambda b, c: (b, c, 0))] + _ssm_weight_specs(),
        out_specs=[pl.BlockSpec((None, lc, SSM_WIDTH), lambda b, c: (b, c, 0)),
                   pl.BlockSpec((1, 1, N_STATE), lambda b, c: (b, 0, 0)),
                   pl.BlockSpec((1, 1, N_STATE), lambda b, c: (b, 0, 0))],
        out_shape=[jax.ShapeDtypeStruct((bsz, seq, SSM_WIDTH), BF16),
                   jax.ShapeDtypeStruct((bsz, 1, N_STATE), F32),
                   jax.ShapeDtypeStruct((bsz, 1, N_STATE), F32)],
        scratch_shapes=[pltpu.VMEM((lc, N_STATE), F32), pltpu.VMEM((lc, N_STATE), F32),
                        pltpu.VMEM((1, N_STATE), F32), pltpu.VMEM((1, N_STATE), F32)],
        compiler_params=_cparams(("parallel", "arbitrary")),
        name="ssm_prompt",
    )(u3, *wts)


def _ssm_sample(u2, s0_re, s0_im, wts, lc):
    t = u2.shape[0]
    nseq = lc // SUBLANES
    return pl.pallas_call(
        _ssm_dec_kernel,
        grid=(t // lc,),
        in_specs=[pl.BlockSpec((lc, SSM_WIDTH), lambda c: (c, 0)),
                  pl.BlockSpec((nseq, N_STATE), lambda c: (c, 0)),
                  pl.BlockSpec((nseq, N_STATE), lambda c: (c, 0))] + _ssm_weight_specs(),
        out_specs=[pl.BlockSpec((lc, SSM_WIDTH), lambda c: (c, 0)),
                   pl.BlockSpec((nseq, N_STATE), lambda c: (c, 0)),
                   pl.BlockSpec((nseq, N_STATE), lambda c: (c, 0))],
        out_shape=[jax.ShapeDtypeStruct((t, SSM_WIDTH), BF16),
                   jax.ShapeDtypeStruct((t // SUBLANES, N_STATE), F32),
                   jax.ShapeDtypeStruct((t // SUBLANES, N_STATE), F32)],
        scratch_shapes=[pltpu.VMEM((lc, N_STATE), F32), pltpu.VMEM((lc, N_STATE), F32)],
        compiler_params=_cparams(("parallel",)),
        name="ssm_sample",
    )(u2, s0_re, s0_im, *wts)


def _head_rms(o, g, lam_init):
    ms = jnp.mean(o * o, axis=-1, keepdims=True)
    return o * lax.rsqrt(ms + LN_EPS) * g * (1.0 - lam_init)


def _attn_kernel(lam_init, blk, slopes_ref, lamp_ref, g_ref, q_ref, k_ref, v_ref, o_ref):
    h = pl.program_id(1)
    qi = pl.program_id(2)
    slope = slopes_ref[h]
    q = q_ref[0]
    lane = lax.broadcasted_iota(jnp.int32, (1, V_DIM), 1)
    zero = jnp.zeros_like(q)
    q1 = jnp.where(lane < HEAD_DIM, q, zero)
    q2 = jnp.where(lane >= HEAD_DIM, q, zero)
    rel = (lax.broadcasted_iota(jnp.int32, (blk, blk), 0)
           - lax.broadcasted_iota(jnp.int32, (blk, blk), 1))
    nt = (((1,), (1,)), ((), ()))

    def body(kj, carry):
        m1, l1, a1, m2, l2, a2 = carry
        ks = pl.multiple_of(kj * blk, blk)
        kb = k_ref[0, pl.ds(ks, blk), :]
        vb = v_ref[0, pl.ds(ks, blk), :]
        dist = rel + (qi - kj) * blk
        bias = slope * dist.astype(F32)
        valid = dist >= 0

        def one(qm, m, l, a):
            s = lax.dot_general(qm, kb, nt, preferred_element_type=F32)
            s = jnp.where(valid, s - bias, NEG_INF)
            mn = jnp.maximum(m, jnp.max(s, axis=-1, keepdims=True))
            p = jnp.exp(s - mn)
            alpha = jnp.exp(m - mn)
            l = alpha * l + jnp.sum(p, axis=-1, keepdims=True)
            a = alpha * a + jnp.dot(p.astype(BF16), vb, preferred_element_type=F32)
            return mn, l, a

        m1, l1, a1 = one(q1, m1, l1, a1)
        m2, l2, a2 = one(q2, m2, l2, a2)
        return m1, l1, a1, m2, l2, a2

    m0 = jnp.full((blk, 1), NEG_INF, F32)
    l0 = jnp.zeros((blk, 1), F32)
    a0 = jnp.zeros((blk, V_DIM), F32)
    m1, l1, a1, m2, l2, a2 = lax.fori_loop(0, qi + 1, body, (m0, l0, a0, m0, l0, a0))
    lam = _diff_lambda(lamp_ref[...], lam_init)
    o = a1 / l1 - lam * (a2 / l2)
    o_ref[0] = _head_rms(o, g_ref[...], lam_init).astype(o_ref.dtype)


def _attn_prompt(q, k, v, slopes, lamp, g, lam_init, blk):
    bsz, seq, _ = q.shape
    kv_spec = pl.BlockSpec((1, seq, V_DIM), lambda b, h, i: (b, 0, h))
    return pl.pallas_call(
        functools.partial(_attn_kernel, lam_init, blk),
        grid=(bsz, N_HEADS, seq // blk),
        in_specs=[pl.BlockSpec(memory_space=pltpu.SMEM),
                  _const_spec((4, HEAD_DIM)), _const_spec((1, V_DIM)),
                  pl.BlockSpec((1, blk, V_DIM), lambda b, h, i: (b, i, h)), kv_spec, kv_spec],
        out_specs=pl.BlockSpec((1, blk, V_DIM), lambda b, h, i: (b, i, h)),
        out_shape=jax.ShapeDtypeStruct((bsz, seq, ATT_WIDTH), BF16),
        compiler_params=_cparams(("parallel", "parallel", "arbitrary")),
        name="attn_prompt",
    )(slopes, lamp, g, q, k, v)


def _page_matrix(ref):
    return jnp.concatenate([ref[0, :, h, :] for h in range(N_HEADS)], axis=1).astype(BF16)


def _sattn_kernel(lam_init, n_pages, pt_ref, lamp_ref, g_ref, q_ref, kn_ref, vn_ref, *rest):
    del pt_ref
    k_refs = rest[:n_pages]
    v_refs = rest[n_pages:2 * n_pages]
    o_ref = rest[2 * n_pages]
    s_scr = rest[2 * n_pages + 1]
    t_new = q_ref.shape[1]
    n_past = n_pages * PAGE_SIZE
    nl = 2 * N_HEADS * t_new
    nt = (((1,), (1,)), ((), ()))
    tn = (((0,), (0,)), ((), ()))

    c = lax.broadcasted_iota(jnp.int32, (1, nl), 1)
    c_q = c & (t_new - 1)
    c_h = (c >> 3) & (N_HEADS - 1)
    c_m = c >> 6
    slope = jnp.exp2(-(c_h + 1).astype(F32))
    qpos = (n_past + c_q).astype(F32)

    r = lax.broadcasted_iota(jnp.int32, (nl, t_new), 0)
    sel = (r & (t_new - 1)) == lax.broadcasted_iota(jnp.int32, (nl, t_new), 1)
    rep = jnp.dot(sel.astype(BF16), q_ref[0], preferred_element_type=F32)
    rr = lax.broadcasted_iota(jnp.int32, (nl, ATT_WIDTH), 0)
    jj = lax.broadcasted_iota(jnp.int32, (nl, ATT_WIDTH), 1)
    same_head = ((rr >> 3) & (N_HEADS - 1)) == (jj >> 7)
    keep = same_head & ((rr >> 6) == ((jj >> 6) & 1))
    qbd = jnp.where(keep, rep, 0.0).astype(BF16)

    row = lax.broadcasted_iota(jnp.int32, (PAGE_SIZE, 1), 0).astype(F32)
    m = jnp.full((1, nl), NEG_INF, F32)
    for i in range(n_pages):
        s = lax.dot_general(_page_matrix(k_refs[i]), qbd, nt, preferred_element_type=F32)
        s = s - slope * (qpos - (row + float(i * PAGE_SIZE)))
        s_scr[i * PAGE_SIZE:(i + 1) * PAGE_SIZE, :] = s
        m = jnp.maximum(m, jnp.max(s, axis=0, keepdims=True))
    dist = c_q - lax.broadcasted_iota(jnp.int32, (t_new, 1), 0)
    sn = lax.dot_general(kn_ref[0].astype(BF16), qbd, nt, preferred_element_type=F32)
    sn = jnp.where(dist >= 0, sn - slope * dist.astype(F32), NEG_INF)
    m = jnp.maximum(m, jnp.max(sn, axis=0, keepdims=True))

    l = jnp.zeros((1, nl), F32)
    for i in range(n_pages):
        p = jnp.exp(s_scr[i * PAGE_SIZE:(i + 1) * PAGE_SIZE, :] - m)
        s_scr[i * PAGE_SIZE:(i + 1) * PAGE_SIZE, :] = p
        l = l + jnp.sum(p, axis=0, keepdims=True)
    pn = jnp.exp(sn - m)
    l = l + jnp.sum(pn, axis=0, keepdims=True)

    lam = _diff_lambda(lamp_ref[...], lam_init)
    coef = jnp.where(c_m == 0, 1.0, -lam) / l
    acc = lax.dot_general((pn * coef).astype(BF16), vn_ref[0].astype(BF16), tn, preferred_element_type=F32)
    for i in range(n_pages):
        w = (s_scr[i * PAGE_SIZE:(i + 1) * PAGE_SIZE, :] * coef).astype(BF16)
        acc = acc + lax.dot_general(w, _page_matrix(v_refs[i]), tn, preferred_element_type=F32)
    acc = jnp.where(same_head, acc, 0.0)
    o = jnp.sum(acc.reshape(nl // t_new, t_new, ATT_WIDTH), axis=0)
    for hh in range(N_HEADS):
        ls = slice(hh * V_DIM, (hh + 1) * V_DIM)
        o_ref[0, :, ls] = _head_rms(o[:, ls], g_ref[...], lam_init).astype(o_ref.dtype)


def _attn_sample(q, kn, vn, cache_k, cache_v, page_table, lamp, g, lam_init):
    nseq, t_new, _ = q.shape
    n_pages = page_table.shape[1]
    tok = pl.BlockSpec((1, t_new, ATT_WIDTH), lambda s, pt: (s, 0, 0))
    page = lambda i: pl.BlockSpec((1, PAGE_SIZE, N_HEADS, V_DIM), lambda s, pt: (pt[s, i], 0, 0, 0))
    grid_spec = pltpu.PrefetchScalarGridSpec(
        num_scalar_prefetch=1,
        grid=(nseq,),
        in_specs=[pl.BlockSpec((4, HEAD_DIM), lambda s, pt: (0, 0)), pl.BlockSpec((1, V_DIM), lambda s, pt: (0, 0)),
                  tok, tok, tok]
                 + [page(i) for i in range(n_pages)] + [page(i) for i in range(n_pages)],
        out_specs=tok,
        scratch_shapes=[pltpu.VMEM((n_pages * PAGE_SIZE, 2 * N_HEADS * t_new), F32)],
    )
    return pl.pallas_call(
        functools.partial(_sattn_kernel, lam_init, n_pages),
        grid_spec=grid_spec,
        out_shape=jax.ShapeDtypeStruct((nseq, t_new, ATT_WIDTH), BF16),
        compiler_params=_cparams(("arbitrary",)),
        name="attn_sample",
    )(page_table, lamp, g, q, kn, vn, *([cache_k] * n_pages), *([cache_v] * n_pages))


def _merge_kernel(x_ref, ssm_ref, att_ref, wg_ref, bg_ref, pssm_ref, patt_ref, wo_ref, g_ref, b_ref, o_ref):
    x = x_ref[...]
    gates = jax.nn.sigmoid(jnp.dot(x.astype(BF16), wg_ref[...], preferred_element_type=F32) + bg_ref[...])
    merged = (gates[:, :D_MODEL] * jnp.dot(ssm_ref[...], pssm_ref[...], preferred_element_type=F32)
              + gates[:, D_MODEL:] * jnp.dot(att_ref[...], patt_ref[...], preferred_element_type=F32))
    y = DN_ALPHA * x + jnp.dot(merged.astype(BF16), wo_ref[...], preferred_element_type=F32)
    o_ref[...] = _layer_norm(y, g_ref[...], b_ref[...])


def _merge(x2d, ssm_y, att, wg, bg, pssm, patt, wo, g, b, tm):
    t = x2d.shape[0]
    row = lambda n: pl.BlockSpec((tm, n), lambda i: (i, 0))
    return pl.pallas_call(
        _merge_kernel,
        grid=(t // tm,),
        in_specs=[row(D_MODEL), row(SSM_WIDTH), row(ATT_WIDTH),
                  _const_spec((D_MODEL, 2 * D_MODEL)), _const_spec((1, 2 * D_MODEL)),
                  _const_spec((SSM_WIDTH, D_MODEL)), _const_spec((ATT_WIDTH, D_MODEL)),
                  _const_spec((D_MODEL, D_MODEL)), _const_spec((1, D_MODEL)), _const_spec((1, D_MODEL))],
        out_specs=row(D_MODEL),
        out_shape=jax.ShapeDtypeStruct((t, D_MODEL), F32),
        compiler_params=_cparams(("parallel",)),
        name="merge",
    )(x2d, ssm_y, att, wg, bg, pssm, patt, wo, g, b)


def _route(x_bf16, wrg, brg, wre, bre):
    gl = jnp.dot(x_bf16, wrg, preferred_element_type=F32) + brg
    el = jnp.dot(x_bf16, wre, preferred_element_type=F32) + bre
    gi = lax.broadcasted_iota(jnp.int32, gl.shape, 1).astype(F32)
    gmax = jnp.max(gl, axis=-1, keepdims=True)
    g_sel = jnp.min(jnp.where(gl == gmax, gi, float(N_EXPERT_GROUPS)), axis=-1, keepdims=True)
    p_g = 1.0 / jnp.sum(jnp.exp(gl - gmax), axis=-1, keepdims=True)
    ei_int = lax.broadcasted_iota(jnp.int32, el.shape, 1)
    ei = ei_int.astype(F32)
    el = jnp.where((ei_int >> 3).astype(F32) == g_sel, el, -jnp.inf)
    v1 = jnp.max(el, axis=-1, keepdims=True)
    i1 = jnp.min(jnp.where(el == v1, ei, float(N_EXPERTS)), axis=-1, keepdims=True)
    el2 = jnp.where(ei == i1, -jnp.inf, el)
    v2 = jnp.max(el2, axis=-1, keepdims=True)
    i2 = jnp.min(jnp.where(el2 == v2, ei, float(N_EXPERTS)), axis=-1, keepdims=True)
    e2 = jnp.exp(v2 - v1)
    w1 = p_g / (1.0 + e2)
    w2 = p_g * e2 / (1.0 + e2)
    return jnp.where(ei == i1, w1, 0.0) + jnp.where(ei == i2, w2, 0.0)


def _moe_kernel(x_ref, wrg_ref, brg_ref, wre_ref, bre_ref, wg_ref, wu_ref, wd_ref, g_ref, b_ref, o_ref,
                xb_scr, comb_scr, acc_scr):
    e = pl.program_id(1)

    @pl.when(e == 0)
    def _():
        xb = x_ref[...].astype(BF16)
        xb_scr[...] = xb
        comb_scr[...] = _route(xb, wrg_ref[...], brg_ref[...], wre_ref[...], bre_ref[...])
        acc_scr[...] = jnp.zeros_like(acc_scr)

    xb = xb_scr[...]
    hg = jnp.dot(xb, wg_ref[0], preferred_element_type=F32)
    hu = jnp.dot(xb, wu_ref[0], preferred_element_type=F32)
    comb = comb_scr[...]
    lane = lax.broadcasted_iota(jnp.int32, comb.shape, 1)
    ce = jnp.sum(jnp.where(lane == e, comb, 0.0), axis=-1, keepdims=True)
    hid = (hg * jax.nn.sigmoid(hg)) * hu * ce
    acc_scr[...] += jnp.dot(hid.astype(BF16), wd_ref[0], preferred_element_type=F32)

    @pl.when(e == N_EXPERTS - 1)
    def _():
        o_ref[...] = _layer_norm(DN_ALPHA * x_ref[...] + acc_scr[...], g_ref[...], b_ref[...])


def _moe(x1, wrg, brg, wre, bre, wg, wu, wd, g, b, tm):
    t = x1.shape[0]
    row = pl.BlockSpec((tm, D_MODEL), lambda i, e: (i, 0))
    return pl.pallas_call(
        _moe_kernel,
        grid=(t // tm, N_EXPERTS),
        in_specs=[row,
                  _const_spec((D_MODEL, N_EXPERT_GROUPS)), _const_spec((1, N_EXPERT_GROUPS)),
                  _const_spec((D_MODEL, N_EXPERTS)), _const_spec((1, N_EXPERTS)),
                  pl.BlockSpec((1, D_MODEL, EXPERT_FF), lambda i, e: (e, 0, 0)),
                  pl.BlockSpec((1, D_MODEL, EXPERT_FF), lambda i, e: (e, 0, 0)),
                  pl.BlockSpec((1, EXPERT_FF, D_MODEL), lambda i, e: (e, 0, 0)),
                  _const_spec((1, D_MODEL)), _const_spec((1, D_MODEL))],
        out_specs=row,
        out_shape=jax.ShapeDtypeStruct((t, D_MODEL), F32),
        scratch_shapes=[pltpu.VMEM((tm, D_MODEL), BF16), pltpu.VMEM((tm, N_EXPERTS), F32),
                        pltpu.VMEM((tm, D_MODEL), F32)],
        compiler_params=_cparams(("parallel", "arbitrary")),
        name="moe",
    )(x1, wrg, brg, wre, bre, wg, wu, wd, g, b)


def kernel(x_prompt, x_sample, cache_k, cache_v, state_ssm_re, state_ssm_im, page_table, w_in, b_gate, ssm_lambda_re, ssm_lambda_im, ssm_log_dt, ssm_b_re, ssm_b_im, ssm_c_re, ssm_c_im, ssm_d, w_glu, b_glu, lambda_q1, lambda_k1, lambda_q2, lambda_k2, subln_g, p_ssm, p_att, w_o, ln1_g, ln1_b, w_router_group, b_router_group, w_router_expert, b_router_expert, w_exp_gate, w_exp_up, w_exp_down, ln2_g, ln2_b):
    bsz, seq, _ = x_prompt.shape
    nseq, t_new, _ = x_sample.shape
    assert w_in.shape[0] == DEPTH == 1 and t_new == SUBLANES
    l = 0
    lam_init = _lambda_init(l)
    row = lambda a: a[l].reshape(1, -1)

    w_qkv = w_in[l][:, :C_G].astype(BF16)
    w_gate = w_in[l][:, C_G:].astype(BF16)
    ssm_tabs = _ssm_tables(ssm_lambda_re[l], ssm_lambda_im[l], ssm_log_dt[l], ssm_b_re[l], ssm_b_im[l],
                           ssm_c_re[l], ssm_c_im[l])
    bre, bim, cre, ncim, mult = ssm_tabs
    ssm_wts = (bre, bim, cre, ncim, row(ssm_d), w_glu[l].astype(BF16), row(b_glu), mult)
    lamp = jnp.stack([lambda_q1[l], lambda_k1[l], lambda_q2[l], lambda_k2[l]])
    g_sub = row(subln_g)
    slopes = jnp.exp2(-8.0 * (jnp.arange(N_HEADS, dtype=F32) + 1.0) / N_HEADS)
    merge_wts = (w_gate, row(b_gate), p_ssm[l].astype(BF16), p_att[l].astype(BF16), w_o[l].astype(BF16),
                 row(ln1_g), row(ln1_b))
    moe_wts = (w_router_group[l].astype(BF16), row(b_router_group), w_router_expert[l].astype(BF16),
               row(b_router_expert), w_exp_gate[l].astype(BF16), w_exp_up[l].astype(BF16),
               w_exp_down[l].astype(BF16), row(ln2_g), row(ln2_b))

    tp = bsz * seq
    xp = x_prompt.reshape(tp, D_MODEL)
    u, q, kf, kb, vf, vb = _project(xp, w_qkv, min(512, tp))
    ssm_y, srp, sip = _ssm_prompt(u.reshape(bsz, seq, SSM_WIDTH), ssm_wts, min(512, seq))
    att = _attn_prompt(q.reshape(bsz, seq, ATT_WIDTH), kb.reshape(bsz, seq, ATT_WIDTH),
                       vb.reshape(bsz, seq, ATT_WIDTH), slopes, lamp, g_sub, lam_init, min(256, seq))
    x1 = _merge(xp, ssm_y.reshape(tp, SSM_WIDTH), att.reshape(tp, ATT_WIDTH), *merge_wts, min(256, tp))
    y_prompt = _moe(x1, *moe_wts, min(1024, tp)).reshape(bsz, seq, D_MODEL)
    k_prompt = kf.reshape(1, bsz, seq, N_HEADS, V_DIM)
    v_prompt = vf.reshape(1, bsz, seq, N_HEADS, V_DIM)
    srp = srp.reshape(1, bsz, SSM_GROUPS, SSM_STATE)
    sip = sip.reshape(1, bsz, SSM_GROUPS, SSM_STATE)

    ts = nseq * t_new
    xs = x_sample.reshape(ts, D_MODEL)
    u, q, kf, kb, vf, vb = _project(xs, w_qkv, min(512, ts))
    ssm_y, srs, sis = _ssm_sample(u, state_ssm_re[l].reshape(nseq, N_STATE), state_ssm_im[l].reshape(nseq, N_STATE),
                                  ssm_wts, min(512, ts))
    att = _attn_sample(q.reshape(nseq, t_new, ATT_WIDTH), kb.reshape(nseq, t_new, ATT_WIDTH),
                       vb.reshape(nseq, t_new, ATT_WIDTH),
                       cache_k[l], cache_v[l], page_table, lamp, g_sub, lam_init)
    x1 = _merge(xs, ssm_y, att.reshape(ts, ATT_WIDTH), *merge_wts, min(256, ts))
    y_sample = _moe(x1, *moe_wts, min(1024, ts)).reshape(nseq, t_new, D_MODEL)
    k_sample = kf.reshape(1, nseq, t_new, N_HEADS, V_DIM)
    v_sample = vf.reshape(1, nseq, t_new, N_HEADS, V_DIM)
    srs = srs.reshape(1, nseq, SSM_GROUPS, SSM_STATE)
    sis = sis.reshape(1, nseq, SSM_GROUPS, SSM_STATE)

    return (y_prompt, y_sample, k_prompt, v_prompt, srp, sip, k_sample, v_sample, srs, sis)
```

```python
import functools
import math

import jax
import jax.numpy as jnp
from jax import lax
from jax.experimental import pallas as pl
from jax.experimental.pallas import tpu as pltpu

F32 = jnp.float32
BF16 = jnp.bfloat16

D_MODEL = 1024
N_HEADS = 8
HEAD_DIM = 64
V_DIM = 2 * HEAD_DIM
ATT_WIDTH = N_HEADS * V_DIM
SSM_WIDTH = 512
SSM_GROUP = 16
SSM_GROUPS = SSM_WIDTH // SSM_GROUP
SSM_STATE = 64
N_STATE = SSM_GROUPS * SSM_STATE
N_EXPERT_GROUPS = 4
EXPERTS_PER_GROUP = 8
N_EXPERTS = N_EXPERT_GROUPS * EXPERTS_PER_GROUP
EXPERT_FF = 256
PAGE_SIZE = 128
DEPTH = 1
DN_ALPHA = (2.0 * DEPTH) ** 0.25
LN_EPS = 1e-5
NEG_INF = -1e30
QK_SCALE = HEAD_DIM ** -0.5
LOG2E = math.log2(math.e)
C_Q = SSM_WIDTH
C_K = C_Q + ATT_WIDTH
C_V = C_K + ATT_WIDTH
C_G = C_V + ATT_WIDTH

VMEM_LIMIT_BYTES = 52 * 1024 * 1024
SUBLANES = 8


def _cparams(sem):
    return pltpu.CompilerParams(dimension_semantics=sem, vmem_limit_bytes=VMEM_LIMIT_BYTES)


def _const_spec(shape):
    nd = len(shape)
    return pl.BlockSpec(shape, lambda *_: (0,) * nd)


def _layer_norm(x, g, b):
    mu = jnp.mean(x, axis=-1, keepdims=True)
    xc = x - mu
    var = jnp.mean(xc * xc, axis=-1, keepdims=True)
    return xc * lax.rsqrt(var + LN_EPS) * g + b


def _lambda_init(layer_idx):
    return 0.8 - 0.6 * math.exp(-0.3 * layer_idx)


def _diff_lambda(lamp, lam_init):
    a = jnp.sum(lamp[0:1, :] * lamp[1:2, :], axis=-1, keepdims=True)
    b = jnp.sum(lamp[2:3, :] * lamp[3:4, :], axis=-1, keepdims=True)
    return jnp.exp(a) - jnp.exp(b) + lam_init


def _store_heads(ref, val):
    n = val.shape[0]
    for h in range(N_HEADS):
        ref[pl.ds(h, n, stride=N_HEADS), :] = val[:, h * V_DIM:(h + 1) * V_DIM]


def _proj_rows_kernel(x_ref, w_ref, u_ref, q_ref, kf_ref, kb_ref, vf_ref, vb_ref):
    x = x_ref[...].astype(BF16)
    u_ref[...] = jnp.dot(x, w_ref[:, 0:C_Q], preferred_element_type=F32)
    q = jnp.dot(x, w_ref[:, C_Q:C_K], preferred_element_type=F32)
    q_ref[...] = (q * QK_SCALE).astype(BF16)
    k = jnp.dot(x, w_ref[:, C_K:C_V], preferred_element_type=F32)
    kb_ref[...] = k.astype(BF16)
    _store_heads(kf_ref, k)
    v = jnp.dot(x, w_ref[:, C_V:C_G], preferred_element_type=F32)
    vb_ref[...] = v.astype(BF16)
    _store_heads(vf_ref, v)


def _project_rows(x2d, w_bf16, tm):
    t = x2d.shape[0]
    row = lambda n: pl.BlockSpec((tm, n), lambda i: (i, 0))
    heads = pl.BlockSpec((tm * N_HEADS, V_DIM), lambda i: (i, 0))
    return pl.pallas_call(
        _proj_rows_kernel,
        grid=(t // tm,),
        in_specs=[row(D_MODEL), _const_spec((D_MODEL, C_G))],
        out_specs=[row(SSM_WIDTH), row(ATT_WIDTH), heads, row(ATT_WIDTH), heads, row(ATT_WIDTH)],
        out_shape=[
            jax.ShapeDtypeStruct((t, SSM_WIDTH), F32),
            jax.ShapeDtypeStruct((t, ATT_WIDTH), BF16),
            jax.ShapeDtypeStruct((t * N_HEADS, V_DIM), F32),
            jax.ShapeDtypeStruct((t, ATT_WIDTH), BF16),
            jax.ShapeDtypeStruct((t * N_HEADS, V_DIM), F32),
            jax.ShapeDtypeStruct((t, ATT_WIDTH), BF16),
        ],
        compiler_params=_cparams(("parallel",)),
        name="proj_rows",
    )(x2d, w_bf16)


def _proj_prompt_kernel(ablk, x_ref, w_ref, wt_ref, u_ref, kf_ref, kb_ref, vf_ref, qt_ref, vt_ref):
    x = x_ref[...].astype(BF16)
    u_ref[...] = jnp.dot(x, w_ref[:, 0:C_Q], preferred_element_type=F32)
    k = jnp.dot(x, w_ref[:, C_Q:C_Q + ATT_WIDTH], preferred_element_type=F32)
    kb_ref[...] = k.astype(BF16)
    _store_heads(kf_ref, k)
    _store_heads(vf_ref, jnp.dot(x, w_ref[:, C_Q + ATT_WIDTH:], preferred_element_type=F32))
    nt = (((1,), (1,)), ((), ()))
    qt = lax.dot_general(wt_ref[0:ATT_WIDTH, :], x, nt, preferred_element_type=F32)
    qt = (qt * (QK_SCALE * LOG2E)).astype(BF16)
    vt = lax.dot_general(wt_ref[ATT_WIDTH:, :], x, nt, preferred_element_type=F32).astype(BF16)
    for j in range(x.shape[0] // ablk):
        qt_ref[j] = qt[:, j * ablk:(j + 1) * ablk]
        vt_ref[j] = vt[:, j * ablk:(j + 1) * ablk]


def _project_prompt(x2d, w_rows, w_cols, tm, ablk):
    t = x2d.shape[0]
    row = lambda n: pl.BlockSpec((tm, n), lambda i: (i, 0))
    heads = pl.BlockSpec((tm * N_HEADS, V_DIM), lambda i: (i, 0))
    tblk = pl.BlockSpec((tm // ablk, ATT_WIDTH, ablk), lambda i: (i, 0, 0))
    return pl.pallas_call(
        functools.partial(_proj_prompt_kernel, ablk),
        grid=(t // tm,),
        in_specs=[row(D_MODEL), _const_spec(w_rows.shape), _const_spec(w_cols.shape)],
        out_specs=[row(SSM_WIDTH), heads, row(ATT_WIDTH), heads, tblk, tblk],
        out_shape=[
            jax.ShapeDtypeStruct((t, SSM_WIDTH), F32),
            jax.ShapeDtypeStruct((t * N_HEADS, V_DIM), F32),
            jax.ShapeDtypeStruct((t, ATT_WIDTH), BF16),
            jax.ShapeDtypeStruct((t * N_HEADS, V_DIM), F32),
            jax.ShapeDtypeStruct((t // ablk, ATT_WIDTH, ablk), BF16),
            jax.ShapeDtypeStruct((t // ablk, ATT_WIDTH, ablk), BF16),
        ],
        compiler_params=_cparams(("parallel",)),
        name="proj_prompt",
    )(x2d, w_rows, w_cols)


def _ssm_tables(lam_re, lam_im, log_dt, b_re, b_im, c_re, c_im):
    dt = jnp.exp(log_dt)[:, None]
    mag = jnp.exp(lam_re * dt)
    a_re, a_im = mag * jnp.cos(lam_im * dt), mag * jnp.sin(lam_im * dt)
    den = lam_re * lam_re + lam_im * lam_im
    zr, zi = a_re - 1.0, a_im
    fr = (zr * lam_re + zi * lam_im) / den
    fi = (zi * lam_re - zr * lam_im) / den
    bb_re = fr[..., None] * b_re - fi[..., None] * b_im
    bb_im = fr[..., None] * b_im + fi[..., None] * b_re
    eye = jnp.eye(SSM_GROUPS, dtype=F32)
    blk_b = lambda m: jnp.einsum('gpc,gh->gchp', m, eye).reshape(SSM_WIDTH, N_STATE).astype(BF16)
    blk_c = lambda m: jnp.einsum('gcp,gh->gphc', m, eye).reshape(N_STATE, SSM_WIDTH).astype(BF16)
    ar, ai = a_re.reshape(1, N_STATE), a_im.reshape(1, N_STATE)

    def cmul(xr, xi, yr, yi):
        return xr * yr - xi * yi, xr * yi + xi * yr

    rows = lax.broadcasted_iota(jnp.int32, (SUBLANES, 1), 0)
    tabs = []
    pr, pi = ar, ai
    for k in range(3):
        keep = rows >= (1 << k)
        tabs += [jnp.where(keep, pr, 0.0), jnp.where(keep, pi, 0.0)]
        pr, pi = cmul(pr, pi, pr, pi)
    cr, ci = [ar], [ai]
    for _ in range(SUBLANES - 1):
        nr, ni = cmul(cr[-1], ci[-1], ar, ai)
        cr.append(nr)
        ci.append(ni)
    tabs += [jnp.concatenate(cr, axis=0), jnp.concatenate(ci, axis=0)]
    return blk_b(bb_re), blk_b(bb_im), blk_c(c_re), blk_c(-c_im), jnp.stack(tabs)


SCAN_LANES = 512


def _scan_rows(sre, sim, mult_ref, r0, carry_re, carry_im):
    out_re, out_im = [], []
    for c0 in range(0, N_STATE, SCAN_LANES):
        ls = slice(c0, c0 + SCAN_LANES)
        xr = sre[pl.ds(r0, SUBLANES), ls]
        xi = sim[pl.ds(r0, SUBLANES), ls]
        for k in range(3):
            d = 1 << k
            yr = pltpu.roll(xr, d, 0)
            yi = pltpu.roll(xi, d, 0)
            mr = mult_ref[2 * k, :, ls]
            mi = mult_ref[2 * k + 1, :, ls]
            xr, xi = xr + (mr * yr - mi * yi), xi + (mr * yi + mi * yr)
        pr = mult_ref[6, :, ls]
        pi = mult_ref[7, :, ls]
        cr = carry_re[:, ls]
        ci = carry_im[:, ls]
        xr, xi = xr + (pr * cr - pi * ci), xi + (pr * ci + pi * cr)
        sre[pl.ds(r0, SUBLANES), ls] = xr
        sim[pl.ds(r0, SUBLANES), ls] = xi
        out_re.append(xr[SUBLANES - 1:SUBLANES, :])
        out_im.append(xi[SUBLANES - 1:SUBLANES, :])
    return jnp.concatenate(out_re, axis=1), jnp.concatenate(out_im, axis=1)


def _gelu_tanh(x):
    return 0.5 * x * (1.0 + jnp.tanh(math.sqrt(2.0 / math.pi) * (x + 0.044715 * (x * x * x))))


def _ssm_head(u_ref, bre_ref, bim_ref, sre, sim):
    ub = u_ref[...].astype(BF16)
    sre[...] = jnp.dot(ub, bre_ref[...], preferred_element_type=F32)
    sim[...] = jnp.dot(ub, bim_ref[...], preferred_element_type=F32)


def _ssm_tail(u_ref, cre_ref, ncim_ref, d_ref, wglu_ref, bglu_ref, y_ref, sre, sim):
    y = (jnp.dot(sre[...].astype(BF16), cre_ref[...], preferred_element_type=F32)
         + jnp.dot(sim[...].astype(BF16), ncim_ref[...], preferred_element_type=F32)
         + d_ref[...] * u_ref[...])
    y = _gelu_tanh(y)
    z = jnp.dot(y.astype(BF16), wglu_ref[...], preferred_element_type=F32) + bglu_ref[...]
    y_ref[...] = (y * jax.nn.sigmoid(z)).astype(BF16)


def _ssm_seq_kernel(u_ref, bre_ref, bim_ref, cre_ref, ncim_ref, d_ref, wglu_ref, bglu_ref, mult_ref,
                    y_ref, ore_ref, oim_ref, sre, sim, car_re, car_im):
    @pl.when(pl.program_id(1) == 0)
    def _():
        car_re[...] = jnp.zeros_like(car_re)
        car_im[...] = jnp.zeros_like(car_im)

    _ssm_head(u_ref, bre_ref, bim_ref, sre, sim)
    n_blk = u_ref.shape[0] // SUBLANES

    def body(n, carry):
        r0 = pl.multiple_of(n * SUBLANES, SUBLANES)
        return _scan_rows(sre, sim, mult_ref, r0, carry[0], carry[1])

    cr, ci = lax.fori_loop(0, n_blk, body, (car_re[...], car_im[...]))
    car_re[...] = cr
    car_im[...] = ci
    ore_ref[0] = cr
    oim_ref[0] = ci
    _ssm_tail(u_ref, cre_ref, ncim_ref, d_ref, wglu_ref, bglu_ref, y_ref, sre, sim)


def _ssm_dec_kernel(u_ref, s0re_ref, s0im_ref, bre_ref, bim_ref, cre_ref, ncim_ref, d_ref, wglu_ref, bglu_ref,
                    mult_ref, y_ref, ore_ref, oim_ref, sre, sim):
    _ssm_head(u_ref, bre_ref, bim_ref, sre, sim)
    n_blk = u_ref.shape[0] // SUBLANES

    def body(n, _):
        r0 = pl.multiple_of(n * SUBLANES, SUBLANES)
        cr, ci = _scan_rows(sre, sim, mult_ref, r0, s0re_ref[pl.ds(n, 1), :], s0im_ref[pl.ds(n, 1), :])
        ore_ref[pl.ds(n, 1), :] = cr
        oim_ref[pl.ds(n, 1), :] = ci
        return 0

    lax.fori_loop(0, n_blk, body, 0)
    _ssm_tail(u_ref, cre_ref, ncim_ref, d_ref, wglu_ref, bglu_ref, y_ref, sre, sim)


def _ssm_weight_specs():
    return [_const_spec((SSM_WIDTH, N_STATE)), _const_spec((SSM_WIDTH, N_STATE)),
            _const_spec((N_STATE, SSM_WIDTH)), _const_spec((N_STATE, SSM_WIDTH)),
            _const_spec((1, SSM_WIDTH)), _const_spec((SSM_WIDTH, SSM_WIDTH)), _const_spec((1, SSM_WIDTH)),
            _const_spec((8, SUBLANES, N_STATE))]


def _ssm_prompt(u3, wts, lc):
    bsz, seq, _ = u3.shape
    return pl.pallas_call(
        _ssm_seq_kernel,
        grid=(bsz, seq // lc),
        in_specs=[pl.BlockSpec((None, lc, SSM_WIDTH), lambda b, c: (b, c, 0))] + _ssm_weight_specs(),
        out_specs=[pl.BlockSpec((None, lc, SSM_WIDTH), lambda b, c: (b, c, 0)),
                   pl.BlockSpec((1, 1, N_STATE), lambda b, c: (b, 0, 0)),
                   pl.BlockSpec((1, 1, N_STATE), lambda b, c: (b, 0, 0))],
        out_shape=[jax.ShapeDtypeStruct((bsz, seq, SSM_WIDTH), BF16),
                   jax.ShapeDtypeStruct((bsz, 1, N_STATE), F32),
                   jax.ShapeDtypeStruct((bsz, 1, N_STATE), F32)],
        scratch_shapes=[pltpu.VMEM((lc, N_STATE), F32), pltpu.VMEM((lc, N_STATE), F32),
                        pltpu.VMEM((1, N_STATE), F32), pltpu.VMEM((1, N_STATE), F32)],
        compiler_params=_cparams(("parallel", "arbitrary")),
        name="ssm_prompt",
    )(u3, *wts)


def _ssm_sample(u2, s0_re, s0_im, wts, lc):
    t = u2.shape[0]
    nseq = lc // SUBLANES
    return pl.pallas_call(
        _ssm_dec_kernel,
        grid=(t // lc,),
        in_specs=[pl.BlockSpec((lc, SSM_WIDTH), lambda c: (c, 0)),
                  pl.BlockSpec((nseq, N_STATE), lambda c: (c, 0)),
                  pl.BlockSpec((nseq, N_STATE), lambda c: (c, 0))] + _ssm_weight_specs(),
        out_specs=[pl.BlockSpec((lc, SSM_WIDTH), lambda c: (c, 0)),
                   pl.BlockSpec((nseq, N_STATE), lambda c: (c, 0)),
                   pl.BlockSpec((nseq, N_STATE), lambda c: (c, 0))],
        out_shape=[jax.ShapeDtypeStruct((t, SSM_WIDTH), BF16),
                   jax.ShapeDtypeStruct((t // SUBLANES, N_STATE), F32),
                   jax.ShapeDtypeStruct((t // SUBLANES, N_STATE), F32)],
        scratch_shapes=[pltpu.VMEM((lc, N_STATE), F32), pltpu.VMEM((lc, N_STATE), F32)],
        compiler_params=_cparams(("parallel",)),
        name="ssm_sample",
    )(u2, s0_re, s0_im, *wts)


def _head_rms(o, g, lam_init):
    ms = jnp.mean(o * o, axis=-1, keepdims=True)
    return o * lax.rsqrt(ms + LN_EPS) * g * (1.0 - lam_init)


N_BIAS_TERMS = 3
ATTN_UNROLL = 4
ONES_ROWS = 16


def _attn_kernel(lam_init, blk, slopes_ref, lamp_ref, g_ref, qt_ref, k_ref, vt_ref, o_ref,
                 acc_ref, m_ref, a_ref, u_ref, p_ref):
    h = pl.program_id(1)
    qi = pl.program_id(2)
    slope2 = slopes_ref[h] * LOG2E
    two = 2 * blk

    qt = qt_ref[0]
    r = lax.broadcasted_iota(jnp.int32, (V_DIM, blk), 0)
    zero = jnp.zeros_like(qt)
    w_top = jnp.concatenate([jnp.where(r < HEAD_DIM, qt, zero), jnp.where(r >= HEAD_DIM, qt, zero)], axis=1)
    ones_rows = lax.broadcasted_iota(jnp.int32, (V_DIM, two), 0) < N_BIAS_TERMS
    w = jnp.concatenate([w_top, jnp.where(ones_rows, 1.0, 0.0).astype(BF16)], axis=0)

    c = lax.broadcasted_iota(jnp.int32, (blk, V_DIM), 0).astype(F32)
    lane = lax.broadcasted_iota(jnp.int32, (blk, V_DIM), 1)
    rem = slope2 * c
    bias_cols = jnp.zeros((blk, V_DIM), F32)
    for t in range(N_BIAS_TERMS):
        term = rem.astype(BF16).astype(F32)
        bias_cols = jnp.where(lane == t, term, bias_cols)
        rem = rem - term
    bias_cols = bias_cols.astype(BF16)

    ones_rows16 = jnp.ones((ONES_ROWS, blk), BF16)

    m_ref[...] = jnp.full(m_ref.shape, NEG_INF, F32)
    a_ref[...] = jnp.ones(a_ref.shape, F32)
    acc_ref[...] = jnp.zeros(acc_ref.shape, F32)
    p_ref[1] = jnp.zeros((blk, two), BF16)

    def scores(kj, slot):
        ks = pl.multiple_of(kj * blk, blk)
        kaug = jnp.concatenate([k_ref[pl.ds(ks, blk), :], bias_cols], axis=1)
        u_ref[slot] = jnp.dot(kaug, w, preferred_element_type=F32)

    def softmax(kj, slot, masked):
        u = u_ref[slot]
        if masked:
            keep = (lax.broadcasted_iota(jnp.int32, (blk, two), 1) & (blk - 1)) >= lax.broadcasted_iota(
                jnp.int32, (blk, two), 0)
            u = jnp.where(keep, u, NEG_INF)
        off = jnp.full((1, 1), (kj - qi) * blk, jnp.int32).astype(F32) * slope2
        m_old = m_ref[...]
        m_new = jnp.maximum(m_old, jnp.max(u, axis=0, keepdims=True) + off)
        a_ref[...] = jnp.exp2(m_old - m_new)
        m_ref[...] = m_new
        p_ref[slot] = jnp.exp2(u + (off - m_new)).astype(BF16)

    def values(kj, slot):
        vaug = jnp.concatenate([vt_ref[jnp.maximum(kj, 0)], ones_rows16], axis=0)
        acc_ref[...] = a_ref[...] * acc_ref[...] + jnp.dot(vaug, p_ref[slot], preferred_element_type=F32)

    def run(kj, slot):
        values(kj - 1, 1 - slot)
        softmax(kj, slot, False)
        scores(kj + 1, 1 - slot)

    scores(0, 0)

    def body(i, carry):
        for d in range(ATTN_UNROLL):
            run(i * ATTN_UNROLL + d, d & 1)
        return carry

    n_main = qi // ATTN_UNROLL
    lax.fori_loop(0, n_main, body, 0)
    base = n_main * ATTN_UNROLL
    arm = ATTN_UNROLL // 2
    while arm >= 1:
        @pl.when((qi & arm) != 0)
        def _(base=base, arm=arm):
            for d in range(arm):
                run(base + d, d & 1)
        base = base + (qi & arm)
        arm //= 2
    last = qi & 1
    values(qi - 1, 1 - last)
    softmax(qi, last, True)
    values(qi, last)

    lam = _diff_lambda(lamp_ref[...], lam_init)
    acc = acc_ref[...]
    inv = 1.0 / acc[V_DIM:V_DIM + 1, :]
    acc = acc[:V_DIM, :]
    ot = acc[:, :blk] * inv[:, :blk] - lam * (acc[:, blk:] * inv[:, blk:])
    ms = jnp.mean(ot * ot, axis=0, keepdims=True)
    ot = ot * lax.rsqrt(ms + LN_EPS) * (g_ref[...] * (1.0 - lam_init))
    o_ref[0] = ot.T.astype(o_ref.dtype)


def _attn_prompt(qt, k, vt, slopes, lamp, g_col, lam_init, bsz, seq, blk):
    nb = seq // blk
    return pl.pallas_call(
        functools.partial(_attn_kernel, lam_init, blk),
        grid=(bsz, N_HEADS, nb),
        in_specs=[pl.BlockSpec(memory_space=pltpu.SMEM),
                  _const_spec((4, HEAD_DIM)), _const_spec((V_DIM, 1)),
                  pl.BlockSpec((1, V_DIM, blk), lambda b, h, i: (b * nb + i, h, 0)),
                  pl.BlockSpec((seq, V_DIM), lambda b, h, i: (b, h)),
                  pl.BlockSpec((nb, V_DIM, blk), lambda b, h, i: (b, h, 0))],
        out_specs=pl.BlockSpec((1, blk, V_DIM), lambda b, h, i: (b, i, h)),
        out_shape=jax.ShapeDtypeStruct((bsz, seq, ATT_WIDTH), BF16),
        scratch_shapes=[pltpu.VMEM((V_DIM + ONES_ROWS, 2 * blk), F32),
                        pltpu.VMEM((1, 2 * blk), F32), pltpu.VMEM((1, 2 * blk), F32),
                        pltpu.VMEM((2, blk, 2 * blk), F32), pltpu.VMEM((2, blk, 2 * blk), BF16)],
        compiler_params=_cparams(("parallel", "parallel", "arbitrary")),
        name="attn_prompt",
    )(slopes, lamp, g_col, qt, k, vt)


def _page_matrix(ref):
    return jnp.concatenate([ref[0, pl.ds(h, PAGE_SIZE, stride=N_HEADS), :].astype(BF16) for h in range(N_HEADS)],
                           axis=1)


def _sattn_kernel(lam_init, n_pages, pt_ref, lamp_ref, g_ref, q_ref, kn_ref, vn_ref, *rest):
    del pt_ref
    k_refs = rest[:n_pages]
    v_refs = rest[n_pages:2 * n_pages]
    o_ref = rest[2 * n_pages]
    s_scr = rest[2 * n_pages + 1]
    t_new = q_ref.shape[1]
    n_past = n_pages * PAGE_SIZE
    nl = 2 * N_HEADS * t_new
    nt = (((1,), (1,)), ((), ()))
    tn = (((0,), (0,)), ((), ()))

    c = lax.broadcasted_iota(jnp.int32, (1, nl), 1)
    c_q = c & (t_new - 1)
    c_h = (c >> 3) & (N_HEADS - 1)
    c_m = c >> 6
    slope = jnp.exp2(-(c_h + 1).astype(F32))
    qpos = (n_past + c_q).astype(F32)

    r = lax.broadcasted_iota(jnp.int32, (nl, t_new), 0)
    sel = (r & (t_new - 1)) == lax.broadcasted_iota(jnp.int32, (nl, t_new), 1)
    rep = jnp.dot(sel.astype(BF16), q_ref[0], preferred_element_type=F32)
    rr = lax.broadcasted_iota(jnp.int32, (nl, ATT_WIDTH), 0)
    jj = lax.broadcasted_iota(jnp.int32, (nl, ATT_WIDTH), 1)
    same_head = ((rr >> 3) & (N_HEADS - 1)) == (jj >> 7)
    keep = same_head & ((rr >> 6) == ((jj >> 6) & 1))
    qbd = jnp.where(keep, rep, 0.0).astype(BF16)

    row = lax.broadcasted_iota(jnp.int32, (PAGE_SIZE, 1), 0).astype(F32)
    m = jnp.full((1, nl), NEG_INF, F32)
    for i in range(n_pages):
        s = lax.dot_general(_page_matrix(k_refs[i]), qbd, nt, preferred_element_type=F32)
        s = s - slope * (qpos - (row + float(i * PAGE_SIZE)))
        s_scr[i * PAGE_SIZE:(i + 1) * PAGE_SIZE, :] = s
        m = jnp.maximum(m, jnp.max(s, axis=0, keepdims=True))
    dist = c_q - lax.broadcasted_iota(jnp.int32, (t_new, 1), 0)
    sn = lax.dot_general(kn_ref[0].astype(BF16), qbd, nt, preferred_element_type=F32)
    sn = jnp.where(dist >= 0, sn - slope * dist.astype(F32), NEG_INF)
    m = jnp.maximum(m, jnp.max(sn, axis=0, keepdims=True))

    l = jnp.zeros((1, nl), F32)
    for i in range(n_pages):
        p = jnp.exp(s_scr[i * PAGE_SIZE:(i + 1) * PAGE_SIZE, :] - m)
        s_scr[i * PAGE_SIZE:(i + 1) * PAGE_SIZE, :] = p
        l = l + jnp.sum(p, axis=0, keepdims=True)
    pn = jnp.exp(sn - m)
    l = l + jnp.sum(pn, axis=0, keepdims=True)

    lam = _diff_lambda(lamp_ref[...], lam_init)
    coef = jnp.where(c_m == 0, 1.0, -lam) / l
    acc = lax.dot_general((pn * coef).astype(BF16), vn_ref[0].astype(BF16), tn, preferred_element_type=F32)
    for i in range(n_pages):
        w = (s_scr[i * PAGE_SIZE:(i + 1) * PAGE_SIZE, :] * coef).astype(BF16)
        acc = acc + lax.dot_general(w, _page_matrix(v_refs[i]), tn, preferred_element_type=F32)
    acc = jnp.where(same_head, acc, 0.0)
    o = jnp.sum(acc.reshape(nl // t_new, t_new, ATT_WIDTH), axis=0)
    for hh in range(N_HEADS):
        ls = slice(hh * V_DIM, (hh + 1) * V_DIM)
        o_ref[0, :, ls] = _head_rms(o[:, ls], g_ref[...], lam_init).astype(o_ref.dtype)


def _attn_sample(q, kn, vn, cache_k, cache_v, page_table, lamp, g, lam_init):
    nseq, t_new, _ = q.shape
    n_pages = page_table.shape[1]
    tok = pl.BlockSpec((1, t_new, ATT_WIDTH), lambda s, pt: (s, 0, 0))
    page = lambda i: pl.BlockSpec((1, PAGE_SIZE * N_HEADS, V_DIM), lambda s, pt: (pt[s, i], 0, 0))
    grid_spec = pltpu.PrefetchScalarGridSpec(
        num_scalar_prefetch=1,
        grid=(nseq,),
        in_specs=[pl.BlockSpec((4, HEAD_DIM), lambda s, pt: (0, 0)), pl.BlockSpec((1, V_DIM), lambda s, pt: (0, 0)),
                  tok, tok, tok]
                 + [page(i) for i in range(n_pages)] + [page(i) for i in range(n_pages)],
        out_specs=tok,
        scratch_shapes=[pltpu.VMEM((n_pages * PAGE_SIZE, 2 * N_HEADS * t_new), F32)],
    )
    return pl.pallas_call(
        functools.partial(_sattn_kernel, lam_init, n_pages),
        grid_spec=grid_spec,
        out_shape=jax.ShapeDtypeStruct((nseq, t_new, ATT_WIDTH), BF16),
        compiler_params=_cparams(("arbitrary",)),
        name="attn_sample",
    )(page_table, lamp, g, q, kn, vn, *([cache_k] * n_pages), *([cache_v] * n_pages))


def _merge_kernel(x_ref, ssm_ref, att_ref, wg_ref, bg_ref, pssm_ref, patt_ref, wo_ref, g_ref, b_ref, o_ref):
    x = x_ref[...]
    gates = jax.nn.sigmoid(jnp.dot(x.astype(BF16), wg_ref[...], preferred_element_type=F32) + bg_ref[...])
    merged = (gates[:, :D_MODEL] * jnp.dot(ssm_ref[...], pssm_ref[...], preferred_element_type=F32)
              + gates[:, D_MODEL:] * jnp.dot(att_ref[...], patt_ref[...], preferred_element_type=F32))
    y = DN_ALPHA * x + jnp.dot(merged.astype(BF16), wo_ref[...], preferred_element_type=F32)
    o_ref[...] = _layer_norm(y, g_ref[...], b_ref[...])


def _merge(x2d, ssm_y, att, wg, bg, pssm, patt, wo, g, b, tm):
    t = x2d.shape[0]
    row = lambda n: pl.BlockSpec((tm, n), lambda i: (i, 0))
    return pl.pallas_call(
        _merge_kernel,
        grid=(t // tm,),
        in_specs=[row(D_MODEL), row(SSM_WIDTH), row(ATT_WIDTH),
                  _const_spec((D_MODEL, 2 * D_MODEL)), _const_spec((1, 2 * D_MODEL)),
                  _const_spec((SSM_WIDTH, D_MODEL)), _const_spec((ATT_WIDTH, D_MODEL)),
                  _const_spec((D_MODEL, D_MODEL)), _const_spec((1, D_MODEL)), _const_spec((1, D_MODEL))],
        out_specs=row(D_MODEL),
        out_shape=jax.ShapeDtypeStruct((t, D_MODEL), F32),
        compiler_params=_cparams(("parallel",)),
        name="merge",
    )(x2d, ssm_y, att, wg, bg, pssm, patt, wo, g, b)


def _route(x_bf16, wrg, brg, wre, bre):
    gl = jnp.dot(x_bf16, wrg, preferred_element_type=F32) + brg
    el = jnp.dot(x_bf16, wre, preferred_element_type=F32) + bre
    gi = lax.broadcasted_iota(jnp.int32, gl.shape, 1).astype(F32)
    gmax = jnp.max(gl, axis=-1, keepdims=True)
    g_sel = jnp.min(jnp.where(gl == gmax, gi, float(N_EXPERT_GROUPS)), axis=-1, keepdims=True)
    p_g = 1.0 / jnp.sum(jnp.exp(gl - gmax), axis=-1, keepdims=True)
    ei_int = lax.broadcasted_iota(jnp.int32, el.shape, 1)
    ei = ei_int.astype(F32)
    el = jnp.where((ei_int >> 3).astype(F32) == g_sel, el, -jnp.inf)
    v1 = jnp.max(el, axis=-1, keepdims=True)
    i1 = jnp.min(jnp.where(el == v1, ei, float(N_EXPERTS)), axis=-1, keepdims=True)
    el2 = jnp.where(ei == i1, -jnp.inf, el)
    v2 = jnp.max(el2, axis=-1, keepdims=True)
    i2 = jnp.min(jnp.where(el2 == v2, ei, float(N_EXPERTS)), axis=-1, keepdims=True)
    e2 = jnp.exp(v2 - v1)
    w1 = p_g / (1.0 + e2)
    w2 = p_g * e2 / (1.0 + e2)
    return jnp.where(ei == i1, w1, 0.0) + jnp.where(ei == i2, w2, 0.0)


def _moe_kernel(x_ref, wrg_ref, brg_ref, wre_ref, bre_ref, wg_ref, wu_ref, wd_ref, g_ref, b_ref, o_ref,
                xb_scr, comb_scr, acc_scr):
    e = pl.program_id(1)

    @pl.when(e == 0)
    def _():
        xb = x_ref[...].astype(BF16)
        xb_scr[...] = xb
        comb_scr[...] = _route(xb, wrg_ref[...], brg_ref[...], wre_ref[...], bre_ref[...])
        acc_scr[...] = jnp.zeros_like(acc_scr)

    xb = xb_scr[...]
    hg = jnp.dot(xb, wg_ref[0], preferred_element_type=F32)
    hu = jnp.dot(xb, wu_ref[0], preferred_element_type=F32)
    comb = comb_scr[...]
    lane = lax.broadcasted_iota(jnp.int32, comb.shape, 1)
    ce = jnp.sum(jnp.where(lane == e, comb, 0.0), axis=-1, keepdims=True)
    hid = (hg * jax.nn.sigmoid(hg)) * hu * ce
    acc_scr[...] += jnp.dot(hid.astype(BF16), wd_ref[0], preferred_element_type=F32)

    @pl.when(e == N_EXPERTS - 1)
    def _():
        o_ref[...] = _layer_norm(DN_ALPHA * x_ref[...] + acc_scr[...], g_ref[...], b_ref[...])


def _moe(x1, wrg, brg, wre, bre, wg, wu, wd, g, b, tm):
    t = x1.shape[0]
    row = pl.BlockSpec((tm, D_MODEL), lambda i, e: (i, 0))
    return pl.pallas_call(
        _moe_kernel,
        grid=(t // tm, N_EXPERTS),
        in_specs=[row,
                  _const_spec((D_MODEL, N_EXPERT_GROUPS)), _const_spec((1, N_EXPERT_GROUPS)),
                  _const_spec((D_MODEL, N_EXPERTS)), _const_spec((1, N_EXPERTS)),
                  pl.BlockSpec((1, D_MODEL, EXPERT_FF), lambda i, e: (e, 0, 0)),
                  pl.BlockSpec((1, D_MODEL, EXPERT_FF), lambda i, e: (e, 0, 0)),
                  pl.BlockSpec((1, EXPERT_FF, D_MODEL), lambda i, e: (e, 0, 0)),
                  _const_spec((1, D_MODEL)), _const_spec((1, D_MODEL))],
        out_specs=row,
        out_shape=jax.ShapeDtypeStruct((t, D_MODEL), F32),
        scratch_shapes=[pltpu.VMEM((tm, D_MODEL), BF16), pltpu.VMEM((tm, N_EXPERTS), F32),
                        pltpu.VMEM((tm, D_MODEL), F32)],
        compiler_params=_cparams(("parallel", "arbitrary")),
        name="moe",
    )(x1, wrg, brg, wre, bre, wg, wu, wd, g, b)


def kernel(x_prompt, x_sample, cache_k, cache_v, state_ssm_re, state_ssm_im, page_table, w_in, b_gate, ssm_lambda_re, ssm_lambda_im, ssm_log_dt, ssm_b_re, ssm_b_im, ssm_c_re, ssm_c_im, ssm_d, w_glu, b_glu, lambda_q1, lambda_k1, lambda_q2, lambda_k2, subln_g, p_ssm, p_att, w_o, ln1_g, ln1_b, w_router_group, b_router_group, w_router_expert, b_router_expert, w_exp_gate, w_exp_up, w_exp_down, ln2_g, ln2_b):
    bsz, seq, _ = x_prompt.shape
    nseq, t_new, _ = x_sample.shape
    assert w_in.shape[0] == DEPTH == 1 and t_new == SUBLANES
    l = 0
    lam_init = _lambda_init(l)
    row = lambda a: a[l].reshape(1, -1)

    w_qkv = w_in[l][:, :C_G].astype(BF16)
    w_gate = w_in[l][:, C_G:].astype(BF16)
    w_ukv = jnp.concatenate([w_qkv[:, :C_Q], w_qkv[:, C_K:]], axis=1)
    w_qv_t = jnp.concatenate([w_qkv[:, C_Q:C_K], w_qkv[:, C_V:]], axis=1).T
    ssm_tabs = _ssm_tables(ssm_lambda_re[l], ssm_lambda_im[l], ssm_log_dt[l], ssm_b_re[l], ssm_b_im[l],
                           ssm_c_re[l], ssm_c_im[l])
    bre, bim, cre, ncim, mult = ssm_tabs
    ssm_wts = (bre, bim, cre, ncim, row(ssm_d), w_glu[l].astype(BF16), row(b_glu), mult)
    lamp = jnp.stack([lambda_q1[l], lambda_k1[l], lambda_q2[l], lambda_k2[l]])
    g_sub = row(subln_g)
    slopes = jnp.exp2(-8.0 * (jnp.arange(N_HEADS, dtype=F32) + 1.0) / N_HEADS)
    merge_wts = (w_gate, row(b_gate), p_ssm[l].astype(BF16), p_att[l].astype(BF16), w_o[l].astype(BF16),
                 row(ln1_g), row(ln1_b))
    moe_wts = (w_router_group[l].astype(BF16), row(b_router_group), w_router_expert[l].astype(BF16),
               row(b_router_expert), w_exp_gate[l].astype(BF16), w_exp_up[l].astype(BF16),
               w_exp_down[l].astype(BF16), row(ln2_g), row(ln2_b))

    tp = bsz * seq
    xp = x_prompt.reshape(tp, D_MODEL)
    ablk = min(256, seq)
    u, kf, kb, vf, qt, vt = _project_prompt(xp, w_ukv, w_qv_t, min(512, tp), ablk)
    ssm_y, srp, sip = _ssm_prompt(u.reshape(bsz, seq, SSM_WIDTH), ssm_wts, min(512, seq))
    att = _attn_prompt(qt, kb, vt, slopes, lamp, subln_g[l].reshape(V_DIM, 1), lam_init, bsz, seq, ablk)
    x1 = _merge(xp, ssm_y.reshape(tp, SSM_WIDTH), att.reshape(tp, ATT_WIDTH), *merge_wts, min(256, tp))
    y_prompt = _moe(x1, *moe_wts, min(1024, tp)).reshape(bsz, seq, D_MODEL)
    k_prompt = kf.reshape(1, bsz, seq, N_HEADS, V_DIM)
    v_prompt = vf.reshape(1, bsz, seq, N_HEADS, V_DIM)
    srp = srp.reshape(1, bsz, SSM_GROUPS, SSM_STATE)
    sip = sip.reshape(1, bsz, SSM_GROUPS, SSM_STATE)

    ts = nseq * t_new
    xs = x_sample.reshape(ts, D_MODEL)
    u, q, kf, kb, vf, vb = _project_rows(xs, w_qkv, min(512, ts))
    ssm_y, srs, sis = _ssm_sample(u, state_ssm_re[l].reshape(nseq, N_STATE), state_ssm_im[l].reshape(nseq, N_STATE),
                                  ssm_wts, min(512, ts))
    att = _attn_sample(q.reshape(nseq, t_new, ATT_WIDTH), kb.reshape(nseq, t_new, ATT_WIDTH),
                       vb.reshape(nseq, t_new, ATT_WIDTH),
                       cache_k[l].reshape(-1, PAGE_SIZE * N_HEADS, V_DIM),
                       cache_v[l].reshape(-1, PAGE_SIZE * N_HEADS, V_DIM), page_table, lamp, g_sub, lam_init)
    x1 = _merge(xs, ssm_y, att.reshape(ts, ATT_WIDTH), *merge_wts, min(256, ts))
    y_sample = _moe(x1, *moe_wts, min(1024, ts)).reshape(nseq, t_new, D_MODEL)
    k_sample = kf.reshape(1, nseq, t_new, N_HEADS, V_DIM)
    v_sample = vf.reshape(1, nseq, t_new, N_HEADS, V_DIM)
    srs = srs.reshape(1, nseq, SSM_GROUPS, SSM_STATE)
    sis = sis.reshape(1, nseq, SSM_GROUPS, SSM_STATE)

    return (y_prompt, y_sample, k_prompt, v_prompt, srp, sip, k_sample, v_sample, srs, sis)
```

```python
import functools
import math

import jax
import jax.numpy as jnp
from jax import lax
from jax.experimental import pallas as pl
from jax.experimental.pallas import tpu as pltpu

F32 = jnp.float32
BF16 = jnp.bfloat16

D_MODEL = 1024
N_HEADS = 8
HEAD_DIM = 64
V_DIM = 2 * HEAD_DIM
ATT_WIDTH = N_HEADS * V_DIM
SSM_WIDTH = 512
SSM_GROUP = 16
SSM_GROUPS = SSM_WIDTH // SSM_GROUP
SSM_STATE = 64
N_STATE = SSM_GROUPS * SSM_STATE
N_EXPERT_GROUPS = 4
EXPERTS_PER_GROUP = 8
N_EXPERTS = N_EXPERT_GROUPS * EXPERTS_PER_GROUP
EXPERT_FF = 256
PAGE_SIZE = 128
DEPTH = 1
DN_ALPHA = (2.0 * DEPTH) ** 0.25
LN_EPS = 1e-5
NEG_INF = -1e30
QK_SCALE = HEAD_DIM ** -0.5
LOG2E = math.log2(math.e)
C_Q = SSM_WIDTH
C_K = C_Q + ATT_WIDTH
C_V = C_K + ATT_WIDTH
C_G = C_V + ATT_WIDTH

VMEM_LIMIT_BYTES = 52 * 1024 * 1024
SUBLANES = 8


def _cparams(sem):
    return pltpu.CompilerParams(dimension_semantics=sem, vmem_limit_bytes=VMEM_LIMIT_BYTES)


def _const_spec(shape):
    nd = len(shape)
    return pl.BlockSpec(shape, lambda *_: (0,) * nd)


def _layer_norm(x, g, b):
    mu = jnp.mean(x, axis=-1, keepdims=True)
    xc = x - mu
    var = jnp.mean(xc * xc, axis=-1, keepdims=True)
    return xc * lax.rsqrt(var + LN_EPS) * g + b


def _lambda_init(layer_idx):
    return 0.8 - 0.6 * math.exp(-0.3 * layer_idx)


def _diff_lambda(lamp, lam_init):
    a = jnp.sum(lamp[0:1, :] * lamp[1:2, :], axis=-1, keepdims=True)
    b = jnp.sum(lamp[2:3, :] * lamp[3:4, :], axis=-1, keepdims=True)
    return jnp.exp(a) - jnp.exp(b) + lam_init


def _store_heads(ref, val):
    n = val.shape[0]
    for h in range(N_HEADS):
        ref[pl.ds(h, n, stride=N_HEADS), :] = val[:, h * V_DIM:(h + 1) * V_DIM]


def _proj_rows_kernel(x_ref, w_ref, u_ref, q_ref, kf_ref, kb_ref, vf_ref, vb_ref):
    x = x_ref[...].astype(BF16)
    u_ref[...] = jnp.dot(x, w_ref[:, 0:C_Q], preferred_element_type=F32)
    q = jnp.dot(x, w_ref[:, C_Q:C_K], preferred_element_type=F32)
    q_ref[...] = (q * QK_SCALE).astype(BF16)
    k = jnp.dot(x, w_ref[:, C_K:C_V], preferred_element_type=F32)
    kb_ref[...] = k.astype(BF16)
    _store_heads(kf_ref, k)
    v = jnp.dot(x, w_ref[:, C_V:C_G], preferred_element_type=F32)
    vb_ref[...] = v.astype(BF16)
    _store_heads(vf_ref, v)


def _project_rows(x2d, w_bf16, tm):
    t = x2d.shape[0]
    row = lambda n: pl.BlockSpec((tm, n), lambda i: (i, 0))
    heads = pl.BlockSpec((tm * N_HEADS, V_DIM), lambda i: (i, 0))
    return pl.pallas_call(
        _proj_rows_kernel,
        grid=(t // tm,),
        in_specs=[row(D_MODEL), _const_spec((D_MODEL, C_G))],
        out_specs=[row(SSM_WIDTH), row(ATT_WIDTH), heads, row(ATT_WIDTH), heads, row(ATT_WIDTH)],
        out_shape=[
            jax.ShapeDtypeStruct((t, SSM_WIDTH), F32),
            jax.ShapeDtypeStruct((t, ATT_WIDTH), BF16),
            jax.ShapeDtypeStruct((t * N_HEADS, V_DIM), F32),
            jax.ShapeDtypeStruct((t, ATT_WIDTH), BF16),
            jax.ShapeDtypeStruct((t * N_HEADS, V_DIM), F32),
            jax.ShapeDtypeStruct((t, ATT_WIDTH), BF16),
        ],
        compiler_params=_cparams(("parallel",)),
        name="proj_rows",
    )(x2d, w_bf16)


def _proj_prompt_kernel(qblk, vblk, x_ref, w_ref, wt_ref, u_ref, kf_ref, kb_ref, vf_ref, qt_ref, vt_ref):
    x = x_ref[...].astype(BF16)
    u_ref[...] = jnp.dot(x, w_ref[:, 0:C_Q], preferred_element_type=F32)
    k = jnp.dot(x, w_ref[:, C_Q:C_Q + ATT_WIDTH], preferred_element_type=F32)
    kb_ref[...] = k.astype(BF16)
    _store_heads(kf_ref, k)
    _store_heads(vf_ref, jnp.dot(x, w_ref[:, C_Q + ATT_WIDTH:], preferred_element_type=F32))
    nt = (((1,), (1,)), ((), ()))
    qt = lax.dot_general(wt_ref[0:ATT_WIDTH, :], x, nt, preferred_element_type=F32)
    qt = (qt * (QK_SCALE * LOG2E)).astype(BF16)
    vt = lax.dot_general(wt_ref[ATT_WIDTH:, :], x, nt, preferred_element_type=F32).astype(BF16)
    for j in range(x.shape[0] // qblk):
        qt_ref[j] = qt[:, j * qblk:(j + 1) * qblk]
    for j in range(x.shape[0] // vblk):
        vt_ref[j] = vt[:, j * vblk:(j + 1) * vblk]


def _project_prompt(x2d, w_rows, w_cols, tm, qblk, vblk):
    t = x2d.shape[0]
    row = lambda n: pl.BlockSpec((tm, n), lambda i: (i, 0))
    heads = pl.BlockSpec((tm * N_HEADS, V_DIM), lambda i: (i, 0))
    tblk = lambda n: pl.BlockSpec((tm // n, ATT_WIDTH, n), lambda i: (i, 0, 0))
    return pl.pallas_call(
        functools.partial(_proj_prompt_kernel, qblk, vblk),
        grid=(t // tm,),
        in_specs=[row(D_MODEL), _const_spec(w_rows.shape), _const_spec(w_cols.shape)],
        out_specs=[row(SSM_WIDTH), heads, row(ATT_WIDTH), heads, tblk(qblk), tblk(vblk)],
        out_shape=[
            jax.ShapeDtypeStruct((t, SSM_WIDTH), F32),
            jax.ShapeDtypeStruct((t * N_HEADS, V_DIM), F32),
            jax.ShapeDtypeStruct((t, ATT_WIDTH), BF16),
            jax.ShapeDtypeStruct((t * N_HEADS, V_DIM), F32),
            jax.ShapeDtypeStruct((t // qblk, ATT_WIDTH, qblk), BF16),
            jax.ShapeDtypeStruct((t // vblk, ATT_WIDTH, vblk), BF16),
        ],
        compiler_params=_cparams(("parallel",)),
        name="proj_prompt",
    )(x2d, w_rows, w_cols)


def _ssm_tables(lam_re, lam_im, log_dt, b_re, b_im, c_re, c_im):
    dt = jnp.exp(log_dt)[:, None]
    mag = jnp.exp(lam_re * dt)
    a_re, a_im = mag * jnp.cos(lam_im * dt), mag * jnp.sin(lam_im * dt)
    den = lam_re * lam_re + lam_im * lam_im
    zr, zi = a_re - 1.0, a_im
    fr = (zr * lam_re + zi * lam_im) / den
    fi = (zi * lam_re - zr * lam_im) / den
    bb_re = fr[..., None] * b_re - fi[..., None] * b_im
    bb_im = fr[..., None] * b_im + fi[..., None] * b_re
    eye = jnp.eye(SSM_GROUPS, dtype=F32)
    blk_b = lambda m: jnp.einsum('gpc,gh->gchp', m, eye).reshape(SSM_WIDTH, N_STATE).astype(BF16)
    blk_c = lambda m: jnp.einsum('gcp,gh->gphc', m, eye).reshape(N_STATE, SSM_WIDTH).astype(BF16)
    ar, ai = a_re.reshape(1, N_STATE), a_im.reshape(1, N_STATE)

    def cmul(xr, xi, yr, yi):
        return xr * yr - xi * yi, xr * yi + xi * yr

    rows = lax.broadcasted_iota(jnp.int32, (SUBLANES, 1), 0)
    tabs = []
    pr, pi = ar, ai
    for k in range(3):
        keep = rows >= (1 << k)
        tabs += [jnp.where(keep, pr, 0.0), jnp.where(keep, pi, 0.0)]
        pr, pi = cmul(pr, pi, pr, pi)
    cr, ci = [ar], [ai]
    for _ in range(SUBLANES - 1):
        nr, ni = cmul(cr[-1], ci[-1], ar, ai)
        cr.append(nr)
        ci.append(ni)
    tabs += [jnp.concatenate(cr, axis=0), jnp.concatenate(ci, axis=0)]
    return blk_b(bb_re), blk_b(bb_im), blk_c(c_re), blk_c(-c_im), jnp.stack(tabs)


SCAN_LANES = 512


def _scan_rows(sre, sim, mult_ref, r0, carry_re, carry_im):
    out_re, out_im = [], []
    for c0 in range(0, N_STATE, SCAN_LANES):
        ls = slice(c0, c0 + SCAN_LANES)
        xr = sre[pl.ds(r0, SUBLANES), ls]
        xi = sim[pl.ds(r0, SUBLANES), ls]
        for k in range(3):
            d = 1 << k
            yr = pltpu.roll(xr, d, 0)
            yi = pltpu.roll(xi, d, 0)
            mr = mult_ref[2 * k, :, ls]
            mi = mult_ref[2 * k + 1, :, ls]
            xr, xi = xr + (mr * yr - mi * yi), xi + (mr * yi + mi * yr)
        pr = mult_ref[6, :, ls]
        pi = mult_ref[7, :, ls]
        cr = carry_re[:, ls]
        ci = carry_im[:, ls]
        xr, xi = xr + (pr * cr - pi * ci), xi + (pr * ci + pi * cr)
        sre[pl.ds(r0, SUBLANES), ls] = xr
        sim[pl.ds(r0, SUBLANES), ls] = xi
        out_re.append(xr[SUBLANES - 1:SUBLANES, :])
        out_im.append(xi[SUBLANES - 1:SUBLANES, :])
    return jnp.concatenate(out_re, axis=1), jnp.concatenate(out_im, axis=1)


def _gelu_tanh(x):
    return 0.5 * x * (1.0 + jnp.tanh(math.sqrt(2.0 / math.pi) * (x + 0.044715 * (x * x * x))))


def _ssm_head(u_ref, bre_ref, bim_ref, sre, sim):
    ub = u_ref[...].astype(BF16)
    sre[...] = jnp.dot(ub, bre_ref[...], preferred_element_type=F32)
    sim[...] = jnp.dot(ub, bim_ref[...], preferred_element_type=F32)


def _ssm_tail(u_ref, cre_ref, ncim_ref, d_ref, wglu_ref, bglu_ref, y_ref, sre, sim):
    y = (jnp.dot(sre[...].astype(BF16), cre_ref[...], preferred_element_type=F32)
         + jnp.dot(sim[...].astype(BF16), ncim_ref[...], preferred_element_type=F32)
         + d_ref[...] * u_ref[...])
    y = _gelu_tanh(y)
    z = jnp.dot(y.astype(BF16), wglu_ref[...], preferred_element_type=F32) + bglu_ref[...]
    y_ref[...] = (y * jax.nn.sigmoid(z)).astype(BF16)


def _ssm_seq_kernel(u_ref, bre_ref, bim_ref, cre_ref, ncim_ref, d_ref, wglu_ref, bglu_ref, mult_ref,
                    y_ref, ore_ref, oim_ref, sre, sim, car_re, car_im):
    @pl.when(pl.program_id(1) == 0)
    def _():
        car_re[...] = jnp.zeros_like(car_re)
        car_im[...] = jnp.zeros_like(car_im)

    _ssm_head(u_ref, bre_ref, bim_ref, sre, sim)
    n_blk = u_ref.shape[0] // SUBLANES

    def body(n, carry):
        r0 = pl.multiple_of(n * SUBLANES, SUBLANES)
        return _scan_rows(sre, sim, mult_ref, r0, carry[0], carry[1])

    cr, ci = lax.fori_loop(0, n_blk, body, (car_re[...], car_im[...]))
    car_re[...] = cr
    car_im[...] = ci
    ore_ref[0] = cr
    oim_ref[0] = ci
    _ssm_tail(u_ref, cre_ref, ncim_ref, d_ref, wglu_ref, bglu_ref, y_ref, sre, sim)


def _ssm_dec_kernel(u_ref, s0re_ref, s0im_ref, bre_ref, bim_ref, cre_ref, ncim_ref, d_ref, wglu_ref, bglu_ref,
                    mult_ref, y_ref, ore_ref, oim_ref, sre, sim):
    _ssm_head(u_ref, bre_ref, bim_ref, sre, sim)
    n_blk = u_ref.shape[0] // SUBLANES

    def body(n, _):
        r0 = pl.multiple_of(n * SUBLANES, SUBLANES)
        cr, ci = _scan_rows(sre, sim, mult_ref, r0, s0re_ref[pl.ds(n, 1), :], s0im_ref[pl.ds(n, 1), :])
        ore_ref[pl.ds(n, 1), :] = cr
        oim_ref[pl.ds(n, 1), :] = ci
        return 0

    lax.fori_loop(0, n_blk, body, 0)
    _ssm_tail(u_ref, cre_ref, ncim_ref, d_ref, wglu_ref, bglu_ref, y_ref, sre, sim)


def _ssm_weight_specs():
    return [_const_spec((SSM_WIDTH, N_STATE)), _const_spec((SSM_WIDTH, N_STATE)),
            _const_spec((N_STATE, SSM_WIDTH)), _const_spec((N_STATE, SSM_WIDTH)),
            _const_spec((1, SSM_WIDTH)), _const_spec((SSM_WIDTH, SSM_WIDTH)), _const_spec((1, SSM_WIDTH)),
            _const_spec((8, SUBLANES, N_STATE))]


def _ssm_prompt(u3, wts, lc):
    bsz, seq, _ = u3.shape
    return pl.pallas_call(
        _ssm_seq_kernel,
        grid=(bsz, seq // lc),
        in_specs=[pl.BlockSpec((None, lc, SSM_WIDTH), lambda b, c: (b, c, 0))] + _ssm_weight_specs(),
        out_specs=[pl.BlockSpec((None, lc, SSM_WIDTH), lambda b, c: (b, c, 0)),
                   pl.BlockSpec((1, 1, N_STATE), lambda b, c: (b, 0, 0)),
                   pl.BlockSpec((1, 1, N_STATE), lambda b, c: (b, 0, 0))],
        out_shape=[jax.ShapeDtypeStruct((bsz, seq, SSM_WIDTH), BF16),
                   jax.ShapeDtypeStruct((bsz, 1, N_STATE), F32),
                   jax.ShapeDtypeStruct((bsz, 1, N_STATE), F32)],
        scratch_shapes=[pltpu.VMEM((lc, N_STATE), F32), pltpu.VMEM((lc, N_STATE), F32),
                        pltpu.VMEM((1, N_STATE), F32), pltpu.VMEM((1, N_STATE), F32)],
        compiler_params=_cparams(("parallel", "arbitrary")),
        name="ssm_prompt",
    )(u3, *wts)


def _ssm_sample(u2, s0_re, s0_im, wts, lc):
    t = u2.shape[0]
    nseq = lc // SUBLANES
    return pl.pallas_call(
        _ssm_dec_kernel,
        grid=(t // lc,),
        in_specs=[pl.BlockSpec((lc, SSM_WIDTH), lambda c: (c, 0)),
                  pl.BlockSpec((nseq, N_STATE), lambda c: (c, 0)),
                  pl.BlockSpec((nseq, N_STATE), lambda c: (c, 0))] + _ssm_weight_specs(),
        out_specs=[pl.BlockSpec((lc, SSM_WIDTH), lambda c: (c, 0)),
                   pl.BlockSpec((nseq, N_STATE), lambda c: (c, 0)),
                   pl.BlockSpec((nseq, N_STATE), lambda c: (c, 0))],
        out_shape=[jax.ShapeDtypeStruct((t, SSM_WIDTH), BF16),
                   jax.ShapeDtypeStruct((t // SUBLANES, N_STATE), F32),
                   jax.ShapeDtypeStruct((t // SUBLANES, N_STATE), F32)],
        scratch_shapes=[pltpu.VMEM((lc, N_STATE), F32), pltpu.VMEM((lc, N_STATE), F32)],
        compiler_params=_cparams(("parallel",)),
        name="ssm_sample",
    )(u2, s0_re, s0_im, *wts)


def _head_rms(o, g, lam_init):
    ms = jnp.mean(o * o, axis=-1, keepdims=True)
    return o * lax.rsqrt(ms + LN_EPS) * g * (1.0 - lam_init)


N_BIAS_TERMS = 3
ATTN_BQ = 256
ATTN_BK = 256
ATTN_HEADS = 2
ATTN_UNROLL = 4
ONES_ROWS = 16


def _attn_kernel(lam_init, bq, bk, nh, slopes_ref, lamp_ref, g_ref, qt_ref, k_ref, vt_ref, o_ref,
                 acc_ref, m_ref, a_ref, u_ref, p_ref):
    hg = pl.program_id(1)
    qi = pl.program_id(2)
    two = 2 * bq
    n_diag = bq // bk
    heads = range(nh)
    hs = lambda hh: slice(hh * V_DIM, (hh + 1) * V_DIM)

    slope2, w, bias_cols = [], [], []
    r = lax.broadcasted_iota(jnp.int32, (V_DIM, bq), 0)
    ones_rows = lax.broadcasted_iota(jnp.int32, (V_DIM, two), 0) < N_BIAS_TERMS
    c = lax.broadcasted_iota(jnp.int32, (bk, V_DIM), 0).astype(F32)
    lane = lax.broadcasted_iota(jnp.int32, (bk, V_DIM), 1)
    for hh in heads:
        slope2.append(slopes_ref[hg * nh + hh] * LOG2E)
        qt = qt_ref[0, hs(hh), :]
        zero = jnp.zeros_like(qt)
        w_top = jnp.concatenate([jnp.where(r < HEAD_DIM, qt, zero), jnp.where(r >= HEAD_DIM, qt, zero)], axis=1)
        w.append(jnp.concatenate([w_top, jnp.where(ones_rows, 1.0, 0.0).astype(BF16)], axis=0))
        rem = slope2[hh] * c
        cols = jnp.zeros((bk, V_DIM), F32)
        for t in range(N_BIAS_TERMS):
            term = rem.astype(BF16).astype(F32)
            cols = jnp.where(lane == t, term, cols)
            rem = rem - term
        bias_cols.append(cols.astype(BF16))

    ones_rows16 = jnp.ones((ONES_ROWS, bk), BF16)

    m_ref[...] = jnp.full(m_ref.shape, NEG_INF, F32)
    a_ref[...] = jnp.ones(a_ref.shape, F32)
    acc_ref[...] = jnp.zeros(acc_ref.shape, F32)
    for hh in heads:
        p_ref[2 * hh + 1] = jnp.zeros((bk, two), BF16)

    def scores(hh, kj, slot):
        ks = pl.multiple_of(kj * bk, bk)
        kaug = jnp.concatenate([k_ref[pl.ds(ks, bk), hs(hh)], bias_cols[hh]], axis=1)
        u_ref[2 * hh + slot] = jnp.dot(kaug, w[hh], preferred_element_type=F32)

    def softmax(hh, kj, slot, diag):
        u = u_ref[2 * hh + slot]
        if diag is not None:
            keep = (lax.broadcasted_iota(jnp.int32, (bk, two), 1) & (bq - 1)) >= (
                lax.broadcasted_iota(jnp.int32, (bk, two), 0) + diag * bk)
            u = jnp.where(keep, u, NEG_INF)
        off = jnp.full((1, 1), kj * bk - qi * bq, jnp.int32).astype(F32) * slope2[hh]
        m_old = m_ref[hh]
        m_new = jnp.maximum(m_old, jnp.max(u, axis=0, keepdims=True) + off)
        a_ref[hh] = jnp.exp2(m_old - m_new)
        m_ref[hh] = m_new
        p_ref[2 * hh + slot] = jnp.exp2(u + (off - m_new)).astype(BF16)

    def values(hh, kj, slot):
        vaug = jnp.concatenate([vt_ref[jnp.maximum(kj, 0), hs(hh), :], ones_rows16], axis=0)
        acc_ref[hh] = a_ref[hh] * acc_ref[hh] + jnp.dot(vaug, p_ref[2 * hh + slot], preferred_element_type=F32)

    def run(kj, slot):
        for hh in heads:
            values(hh, kj - 1, 1 - slot)
            softmax(hh, kj, slot, None)
            scores(hh, kj + 1, 1 - slot)

    for hh in heads:
        scores(hh, 0, 0)

    def body(i, carry):
        for d in range(ATTN_UNROLL):
            run(i * ATTN_UNROLL + d, d & 1)
        return carry

    n_full = qi * n_diag
    n_main = n_full // ATTN_UNROLL
    lax.fori_loop(0, n_main, body, 0)
    base = n_main * ATTN_UNROLL
    arm = ATTN_UNROLL // 2
    while arm >= 1:
        if n_diag % (2 * arm) != 0:
            @pl.when((n_full & arm) != 0)
            def _(base=base, arm=arm):
                for d in range(arm):
                    run(base + d, d & 1)
            base = base + (n_full & arm)
        arm //= 2
    for d in range(n_diag):
        slot = d & 1 if n_diag % 2 == 0 else (n_full + d) & 1
        for hh in heads:
            values(hh, n_full + d - 1, 1 - slot)
            softmax(hh, n_full + d, slot, d)
            if d + 1 < n_diag:
                scores(hh, n_full + d + 1, 1 - slot)
    lam = _diff_lambda(lamp_ref[...], lam_init)
    for hh in heads:
        values(hh, n_full + n_diag - 1, slot)
        acc = acc_ref[hh]
        inv = 1.0 / acc[V_DIM:V_DIM + 1, :]
        acc = acc[:V_DIM, :]
        ot = acc[:, :bq] * inv[:, :bq] - lam * (acc[:, bq:] * inv[:, bq:])
        ms = jnp.mean(ot * ot, axis=0, keepdims=True)
        ot = ot * lax.rsqrt(ms + LN_EPS) * (g_ref[...] * (1.0 - lam_init))
        o_ref[0, :, hs(hh)] = ot.T.astype(o_ref.dtype)


def _attn_prompt(qt, k, vt, slopes, lamp, g_col, lam_init, bsz, seq, bq, bk, nh):
    nq, nk = seq // bq, seq // bk
    hw = nh * V_DIM
    return pl.pallas_call(
        functools.partial(_attn_kernel, lam_init, bq, bk, nh),
        grid=(bsz, N_HEADS // nh, nq),
        in_specs=[pl.BlockSpec(memory_space=pltpu.SMEM),
                  _const_spec((4, HEAD_DIM)), _const_spec((V_DIM, 1)),
                  pl.BlockSpec((1, hw, bq), lambda b, h, i: (b * nq + i, h, 0)),
                  pl.BlockSpec((seq, hw), lambda b, h, i: (b, h)),
                  pl.BlockSpec((nk, hw, bk), lambda b, h, i: (b, h, 0))],
        out_specs=pl.BlockSpec((1, bq, hw), lambda b, h, i: (b, i, h)),
        out_shape=jax.ShapeDtypeStruct((bsz, seq, ATT_WIDTH), BF16),
        scratch_shapes=[pltpu.VMEM((nh, V_DIM + ONES_ROWS, 2 * bq), F32),
                        pltpu.VMEM((nh, 1, 2 * bq), F32), pltpu.VMEM((nh, 1, 2 * bq), F32),
                        pltpu.VMEM((2 * nh, bk, 2 * bq), F32), pltpu.VMEM((2 * nh, bk, 2 * bq), BF16)],
        compiler_params=_cparams(("parallel", "parallel", "arbitrary")),
        name="attn_prompt",
    )(slopes, lamp, g_col, qt, k, vt)


def _page_matrix(ref):
    return jnp.concatenate([ref[0, pl.ds(h, PAGE_SIZE, stride=N_HEADS), :].astype(BF16) for h in range(N_HEADS)],
                           axis=1)


def _sattn_kernel(lam_init, n_pages, pt_ref, lamp_ref, g_ref, q_ref, kn_ref, vn_ref, *rest):
    del pt_ref
    k_refs = rest[:n_pages]
    v_refs = rest[n_pages:2 * n_pages]
    o_ref = rest[2 * n_pages]
    s_scr = rest[2 * n_pages + 1]
    t_new = q_ref.shape[1]
    n_past = n_pages * PAGE_SIZE
    nl = 2 * N_HEADS * t_new
    nt = (((1,), (1,)), ((), ()))
    tn = (((0,), (0,)), ((), ()))

    c = lax.broadcasted_iota(jnp.int32, (1, nl), 1)
    c_q = c & (t_new - 1)
    c_h = (c >> 3) & (N_HEADS - 1)
    c_m = c >> 6
    slope = jnp.exp2(-(c_h + 1).astype(F32))
    qpos = (n_past + c_q).astype(F32)

    r = lax.broadcasted_iota(jnp.int32, (nl, t_new), 0)
    sel = (r & (t_new - 1)) == lax.broadcasted_iota(jnp.int32, (nl, t_new), 1)
    rep = jnp.dot(sel.astype(BF16), q_ref[0], preferred_element_type=F32)
    rr = lax.broadcasted_iota(jnp.int32, (nl, ATT_WIDTH), 0)
    jj = lax.broadcasted_iota(jnp.int32, (nl, ATT_WIDTH), 1)
    same_head = ((rr >> 3) & (N_HEADS - 1)) == (jj >> 7)
    keep = same_head & ((rr >> 6) == ((jj >> 6) & 1))
    qbd = jnp.where(keep, rep, 0.0).astype(BF16)

    row = lax.broadcasted_iota(jnp.int32, (PAGE_SIZE, 1), 0).astype(F32)
    m = jnp.full((1, nl), NEG_INF, F32)
    for i in range(n_pages):
        s = lax.dot_general(_page_matrix(k_refs[i]), qbd, nt, preferred_element_type=F32)
        s = s - slope * (qpos - (row + float(i * PAGE_SIZE)))
        s_scr[i * PAGE_SIZE:(i + 1) * PAGE_SIZE, :] = s
        m = jnp.maximum(m, jnp.max(s, axis=0, keepdims=True))
    dist = c_q - lax.broadcasted_iota(jnp.int32, (t_new, 1), 0)
    sn = lax.dot_general(kn_ref[0].astype(BF16), qbd, nt, preferred_element_type=F32)
    sn = jnp.where(dist >= 0, sn - slope * dist.astype(F32), NEG_INF)
    m = jnp.maximum(m, jnp.max(sn, axis=0, keepdims=True))

    l = jnp.zeros((1, nl), F32)
    for i in range(n_pages):
        p = jnp.exp(s_scr[i * PAGE_SIZE:(i + 1) * PAGE_SIZE, :] - m)
        s_scr[i * PAGE_SIZE:(i + 1) * PAGE_SIZE, :] = p
        l = l + jnp.sum(p, axis=0, keepdims=True)
    pn = jnp.exp(sn - m)
    l = l + jnp.sum(pn, axis=0, keepdims=True)

    lam = _diff_lambda(lamp_ref[...], lam_init)
    coef = jnp.where(c_m == 0, 1.0, -lam) / l
    acc = lax.dot_general((pn * coef).astype(BF16), vn_ref[0].astype(BF16), tn, preferred_element_type=F32)
    for i in range(n_pages):
        w = (s_scr[i * PAGE_SIZE:(i + 1) * PAGE_SIZE, :] * coef).astype(BF16)
        acc = acc + lax.dot_general(w, _page_matrix(v_refs[i]), tn, preferred_element_type=F32)
    acc = jnp.where(same_head, acc, 0.0)
    o = jnp.sum(acc.reshape(nl // t_new, t_new, ATT_WIDTH), axis=0)
    for hh in range(N_HEADS):
        ls = slice(hh * V_DIM, (hh + 1) * V_DIM)
        o_ref[0, :, ls] = _head_rms(o[:, ls], g_ref[...], lam_init).astype(o_ref.dtype)


def _attn_sample(q, kn, vn, cache_k, cache_v, page_table, lamp, g, lam_init):
    nseq, t_new, _ = q.shape
    n_pages = page_table.shape[1]
    tok = pl.BlockSpec((1, t_new, ATT_WIDTH), lambda s, pt: (s, 0, 0))
    page = lambda i: pl.BlockSpec((1, PAGE_SIZE * N_HEADS, V_DIM), lambda s, pt: (pt[s, i], 0, 0))
    grid_spec = pltpu.PrefetchScalarGridSpec(
        num_scalar_prefetch=1,
        grid=(nseq,),
        in_specs=[pl.BlockSpec((4, HEAD_DIM), lambda s, pt: (0, 0)), pl.BlockSpec((1, V_DIM), lambda s, pt: (0, 0)),
                  tok, tok, tok]
                 + [page(i) for i in range(n_pages)] + [page(i) for i in range(n_pages)],
        out_specs=tok,
        scratch_shapes=[pltpu.VMEM((n_pages * PAGE_SIZE, 2 * N_HEADS * t_new), F32)],
    )
    return pl.pallas_call(
        functools.partial(_sattn_kernel, lam_init, n_pages),
        grid_spec=grid_spec,
        out_shape=jax.ShapeDtypeStruct((nseq, t_new, ATT_WIDTH), BF16),
        compiler_params=_cparams(("arbitrary",)),
        name="attn_sample",
    )(page_table, lamp, g, q, kn, vn, *([cache_k] * n_pages), *([cache_v] * n_pages))


def _merge_kernel(x_ref, ssm_ref, att_ref, wg_ref, bg_ref, pssm_ref, patt_ref, wo_ref, g_ref, b_ref, o_ref):
    x = x_ref[...]
    gates = jax.nn.sigmoid(jnp.dot(x.astype(BF16), wg_ref[...], preferred_element_type=F32) + bg_ref[...])
    merged = (gates[:, :D_MODEL] * jnp.dot(ssm_ref[...], pssm_ref[...], preferred_element_type=F32)
              + gates[:, D_MODEL:] * jnp.dot(att_ref[...], patt_ref[...], preferred_element_type=F32))
    y = DN_ALPHA * x + jnp.dot(merged.astype(BF16), wo_ref[...], preferred_element_type=F32)
    o_ref[...] = _layer_norm(y, g_ref[...], b_ref[...])


def _merge(x2d, ssm_y, att, wg, bg, pssm, patt, wo, g, b, tm):
    t = x2d.shape[0]
    row = lambda n: pl.BlockSpec((tm, n), lambda i: (i, 0))
    return pl.pallas_call(
        _merge_kernel,
        grid=(t // tm,),
        in_specs=[row(D_MODEL), row(SSM_WIDTH), row(ATT_WIDTH),
                  _const_spec((D_MODEL, 2 * D_MODEL)), _const_spec((1, 2 * D_MODEL)),
                  _const_spec((SSM_WIDTH, D_MODEL)), _const_spec((ATT_WIDTH, D_MODEL)),
                  _const_spec((D_MODEL, D_MODEL)), _const_spec((1, D_MODEL)), _const_spec((1, D_MODEL))],
        out_specs=row(D_MODEL),
        out_shape=jax.ShapeDtypeStruct((t, D_MODEL), F32),
        compiler_params=_cparams(("parallel",)),
        name="merge",
    )(x2d, ssm_y, att, wg, bg, pssm, patt, wo, g, b)


def _route(x_bf16, wrg, brg, wre, bre):
    gl = jnp.dot(x_bf16, wrg, preferred_element_type=F32) + brg
    el = jnp.dot(x_bf16, wre, preferred_element_type=F32) + bre
    gi = lax.broadcasted_iota(jnp.int32, gl.shape, 1).astype(F32)
    gmax = jnp.max(gl, axis=-1, keepdims=True)
    g_sel = jnp.min(jnp.where(gl == gmax, gi, float(N_EXPERT_GROUPS)), axis=-1, keepdims=True)
    p_g = 1.0 / jnp.sum(jnp.exp(gl - gmax), axis=-1, keepdims=True)
    ei_int = lax.broadcasted_iota(jnp.int32, el.shape, 1)
    ei = ei_int.astype(F32)
    el = jnp.where((ei_int >> 3).astype(F32) == g_sel, el, -jnp.inf)
    v1 = jnp.max(el, axis=-1, keepdims=True)
    i1 = jnp.min(jnp.where(el == v1, ei, float(N_EXPERTS)), axis=-1, keepdims=True)
    el2 = jnp.where(ei == i1, -jnp.inf, el)
    v2 = jnp.max(el2, axis=-1, keepdims=True)
    i2 = jnp.min(jnp.where(el2 == v2, ei, float(N_EXPERTS)), axis=-1, keepdims=True)
    e2 = jnp.exp(v2 - v1)
    w1 = p_g / (1.0 + e2)
    w2 = p_g * e2 / (1.0 + e2)
    lane = lax.broadcasted_iota(jnp.int32, (gl.shape[0], LANES), 1).astype(F32)
    first = EXPERTS_PER_GROUP * g_sel
    comb = jnp.where(lane == i1 - first, w1, 0.0) + jnp.where(lane == i2 - first, w2, 0.0)
    return jnp.where(lane == g_sel, 1.0, 0.0), comb


LANES = 128
MOE_ROWS = 128
MOE_EXPERTS_PER_STEP = 4


def _moe_kernel(x_ref, wrg_ref, brg_ref, wre_ref, bre_ref, wg_ref, wu_ref, wd_ref, g_ref, b_ref, o_ref,
                xs_scr, comb_scr, acc_scr, unsort_scr, seg_ref):
    e = pl.program_id(1)
    tm = x_ref.shape[0]
    srows = xs_scr.shape[0]
    lane = lax.broadcasted_iota(jnp.int32, (1, LANES), 1)

    @pl.when(e == 0)
    def _():
        xb = x_ref[...].astype(BF16)
        group_1h, comb = _route(xb, wrg_ref[...], brg_ref[...], wre_ref[...], bre_ref[...])
        tri = lax.broadcasted_iota(jnp.int32, (tm, tm), 0) >= lax.broadcasted_iota(jnp.int32, (tm, tm), 1)
        count = jnp.dot(jnp.where(tri, 1.0, 0.0).astype(BF16), group_1h.astype(BF16), preferred_element_type=F32)
        total = count[tm - 1:tm, :]
        padded = jnp.floor((total + (MOE_ROWS - 1)) * (1.0 / MOE_ROWS)) * MOE_ROWS
        start = jnp.zeros((1, LANES), F32)
        for gidx in range(N_EXPERT_GROUPS):
            size = jnp.sum(jnp.where(lane == gidx, padded, 0.0), axis=-1, keepdims=True)
            seg_ref[0, gidx] = jnp.sum(jnp.where(lane == gidx, start, 0.0)).astype(jnp.int32) // MOE_ROWS
            seg_ref[1, gidx] = jnp.sum(size).astype(jnp.int32) // MOE_ROWS
            start = start + jnp.where(lane > gidx, size, 0.0)
        pos = jnp.sum(group_1h * (start + count - 1.0), axis=-1, keepdims=True)
        unsort = jnp.where(lax.broadcasted_iota(jnp.int32, (tm, srows), 1).astype(F32) == pos, 1.0, 0.0)
        unsort_scr[...] = unsort.astype(BF16)
        pos_row = jnp.broadcast_to(pos, (tm, LANES)).T[0:1, :]
        sort = jnp.where(lax.broadcasted_iota(jnp.int32, (srows, tm), 0).astype(F32) == pos_row, 1.0, 0.0)
        sort = sort.astype(BF16)
        xs_scr[...] = jnp.dot(sort, xb, preferred_element_type=F32).astype(BF16)
        comb_hi = comb.astype(BF16)
        comb_lo = (comb - comb_hi.astype(F32)).astype(BF16)
        comb_scr[...] = (jnp.dot(sort, comb_hi, preferred_element_type=F32)
                         + jnp.dot(sort, comb_lo, preferred_element_type=F32))
        acc_scr[...] = jnp.zeros_like(acc_scr)

    first_expert = e * MOE_EXPERTS_PER_STEP
    grp = first_expert >> 3
    in_grp = first_expert & (EXPERTS_PER_GROUP - 1)
    first_blk = seg_ref[0, grp]

    def block(i, carry):
        r0 = pl.multiple_of((first_blk + i) * MOE_ROWS, MOE_ROWS)
        xs = xs_scr[pl.ds(r0, MOE_ROWS), :]
        comb = comb_scr[pl.ds(r0, MOE_ROWS), :]
        out = None
        for k in range(MOE_EXPERTS_PER_STEP):
            hg = jnp.dot(xs, wg_ref[k], preferred_element_type=F32)
            hu = jnp.dot(xs, wu_ref[k], preferred_element_type=F32)
            ce = jnp.sum(jnp.where(lane == in_grp + k, comb, 0.0), axis=-1, keepdims=True)
            hid = (hg * jax.nn.sigmoid(hg)) * hu * ce
            part = jnp.dot(hid.astype(BF16), wd_ref[k], preferred_element_type=F32)
            out = part if out is None else out + part
        acc_scr[pl.ds(r0, MOE_ROWS), :] += out
        return carry

    lax.fori_loop(0, seg_ref[1, grp], block, 0)

    @pl.when(e == N_EXPERTS // MOE_EXPERTS_PER_STEP - 1)
    def _():
        ffn = jnp.dot(unsort_scr[...], acc_scr[...].astype(BF16), preferred_element_type=F32)
        o_ref[...] = _layer_norm(DN_ALPHA * x_ref[...] + ffn, g_ref[...], b_ref[...])


def _moe(x1, wrg, brg, wre, bre, wg, wu, wd, g, b, tm):
    t = x1.shape[0]
    srows = tm + N_EXPERT_GROUPS * MOE_ROWS
    row = pl.BlockSpec((tm, D_MODEL), lambda i, e: (i, 0))
    return pl.pallas_call(
        _moe_kernel,
        grid=(t // tm, N_EXPERTS // MOE_EXPERTS_PER_STEP),
        in_specs=[row,
                  _const_spec((D_MODEL, N_EXPERT_GROUPS)), _const_spec((1, N_EXPERT_GROUPS)),
                  _const_spec((D_MODEL, N_EXPERTS)), _const_spec((1, N_EXPERTS)),
                  pl.BlockSpec((MOE_EXPERTS_PER_STEP, D_MODEL, EXPERT_FF), lambda i, e: (e, 0, 0)),
                  pl.BlockSpec((MOE_EXPERTS_PER_STEP, D_MODEL, EXPERT_FF), lambda i, e: (e, 0, 0)),
                  pl.BlockSpec((MOE_EXPERTS_PER_STEP, EXPERT_FF, D_MODEL), lambda i, e: (e, 0, 0)),
                  _const_spec((1, D_MODEL)), _const_spec((1, D_MODEL))],
        out_specs=row,
        out_shape=jax.ShapeDtypeStruct((t, D_MODEL), F32),
        scratch_shapes=[pltpu.VMEM((srows, D_MODEL), BF16), pltpu.VMEM((srows, LANES), F32),
                        pltpu.VMEM((srows, D_MODEL), F32), pltpu.VMEM((tm, srows), BF16),
                        pltpu.SMEM((2, N_EXPERT_GROUPS), jnp.int32)],
        compiler_params=_cparams(("parallel", "arbitrary")),
        name="moe",
    )(x1, wrg, brg, wre, bre, wg, wu, wd, g, b)


def kernel(x_prompt, x_sample, cache_k, cache_v, state_ssm_re, state_ssm_im, page_table, w_in, b_gate, ssm_lambda_re, ssm_lambda_im, ssm_log_dt, ssm_b_re, ssm_b_im, ssm_c_re, ssm_c_im, ssm_d, w_glu, b_glu, lambda_q1, lambda_k1, lambda_q2, lambda_k2, subln_g, p_ssm, p_att, w_o, ln1_g, ln1_b, w_router_group, b_router_group, w_router_expert, b_router_expert, w_exp_gate, w_exp_up, w_exp_down, ln2_g, ln2_b):
    bsz, seq, _ = x_prompt.shape
    nseq, t_new, _ = x_sample.shape
    assert w_in.shape[0] == DEPTH == 1 and t_new == SUBLANES
    l = 0
    lam_init = _lambda_init(l)
    row = lambda a: a[l].reshape(1, -1)

    w_qkv = w_in[l][:, :C_G].astype(BF16)
    w_gate = w_in[l][:, C_G:].astype(BF16)
    w_ukv = jnp.concatenate([w_qkv[:, :C_Q], w_qkv[:, C_K:]], axis=1)
    w_qv_t = jnp.concatenate([w_qkv[:, C_Q:C_K], w_qkv[:, C_V:]], axis=1).T
    ssm_tabs = _ssm_tables(ssm_lambda_re[l], ssm_lambda_im[l], ssm_log_dt[l], ssm_b_re[l], ssm_b_im[l],
                           ssm_c_re[l], ssm_c_im[l])
    bre, bim, cre, ncim, mult = ssm_tabs
    ssm_wts = (bre, bim, cre, ncim, row(ssm_d), w_glu[l].astype(BF16), row(b_glu), mult)
    lamp = jnp.stack([lambda_q1[l], lambda_k1[l], lambda_q2[l], lambda_k2[l]])
    g_sub = row(subln_g)
    slopes = jnp.exp2(-8.0 * (jnp.arange(N_HEADS, dtype=F32) + 1.0) / N_HEADS)
    merge_wts = (w_gate, row(b_gate), p_ssm[l].astype(BF16), p_att[l].astype(BF16), w_o[l].astype(BF16),
                 row(ln1_g), row(ln1_b))
    moe_wts = (w_router_group[l].astype(BF16), row(b_router_group), w_router_expert[l].astype(BF16),
               row(b_router_expert), w_exp_gate[l].astype(BF16), w_exp_up[l].astype(BF16),
               w_exp_down[l].astype(BF16), row(ln2_g), row(ln2_b))

    tp = bsz * seq
    xp = x_prompt.reshape(tp, D_MODEL)
    bq, bk = min(ATTN_BQ, seq), min(ATTN_BK, seq)
    u, kf, kb, vf, qt, vt = _project_prompt(xp, w_ukv, w_qv_t, min(512, tp), bq, bk)
    ssm_y, srp, sip = _ssm_prompt(u.reshape(bsz, seq, SSM_WIDTH), ssm_wts, min(512, seq))
    att = _attn_prompt(qt, kb, vt, slopes, lamp, subln_g[l].reshape(V_DIM, 1), lam_init, bsz, seq, bq, bk,
                       ATTN_HEADS)
    x1 = _merge(xp, ssm_y.reshape(tp, SSM_WIDTH), att.reshape(tp, ATT_WIDTH), *merge_wts, min(256, tp))
    y_prompt = _moe(x1, *moe_wts, min(1024, tp)).reshape(bsz, seq, D_MODEL)
    k_prompt = kf.reshape(1, bsz, seq, N_HEADS, V_DIM)
    v_prompt = vf.reshape(1, bsz, seq, N_HEADS, V_DIM)
    srp = srp.reshape(1, bsz, SSM_GROUPS, SSM_STATE)
    sip = sip.reshape(1, bsz, SSM_GROUPS, SSM_STATE)

    ts = nseq * t_new
    xs = x_sample.reshape(ts, D_MODEL)
    u, q, kf, kb, vf, vb = _project_rows(xs, w_qkv, min(512, ts))
    ssm_y, srs, sis = _ssm_sample(u, state_ssm_re[l].reshape(nseq, N_STATE), state_ssm_im[l].reshape(nseq, N_STATE),
                                  ssm_wts, min(512, ts))
    att = _attn_sample(q.reshape(nseq, t_new, ATT_WIDTH), kb.reshape(nseq, t_new, ATT_WIDTH),
                       vb.reshape(nseq, t_new, ATT_WIDTH),
                       cache_k[l].reshape(-1, PAGE_SIZE * N_HEADS, V_DIM),
                       cache_v[l].reshape(-1, PAGE_SIZE * N_HEADS, V_DIM), page_table, lamp, g_sub, lam_init)
    x1 = _merge(xs, ssm_y, att.reshape(ts, ATT_WIDTH), *merge_wts, min(256, ts))
    y_sample = _moe(x1, *moe_wts, min(1024, ts)).reshape(nseq, t_new, D_MODEL)
    k_sample = kf.reshape(1, nseq, t_new, N_HEADS, V_DIM)
    v_sample = vf.reshape(1, nseq, t_new, N_HEADS, V_DIM)
    srs = srs.reshape(1, nseq, SSM_GROUPS, SSM_STATE)
    sis = sis.reshape(1, nseq, SSM_GROUPS, SSM_STATE)

    return (y_prompt, y_sample, k_prompt, v_prompt, srp, sip, k_sample, v_sample, srs, sis)
```

```python
import functools
import math

import jax
import jax.numpy as jnp
from jax import lax
from jax.experimental import pallas as pl
from jax.experimental.pallas import tpu as pltpu

F32 = jnp.float32
BF16 = jnp.bfloat16

D_MODEL = 1024
N_HEADS = 8
HEAD_DIM = 64
V_DIM = 2 * HEAD_DIM
ATT_WIDTH = N_HEADS * V_DIM
SSM_WIDTH = 512
SSM_GROUP = 16
SSM_GROUPS = SSM_WIDTH // SSM_GROUP
SSM_STATE = 64
N_STATE = SSM_GROUPS * SSM_STATE
N_EXPERT_GROUPS = 4
EXPERTS_PER_GROUP = 8
N_EXPERTS = N_EXPERT_GROUPS * EXPERTS_PER_GROUP
EXPERT_FF = 256
PAGE_SIZE = 128
DEPTH = 1
DN_ALPHA = (2.0 * DEPTH) ** 0.25
LN_EPS = 1e-5
NEG_INF = -1e30
QK_SCALE = HEAD_DIM ** -0.5
LOG2E = math.log2(math.e)
C_Q = SSM_WIDTH
C_K = C_Q + ATT_WIDTH
C_V = C_K + ATT_WIDTH
C_G = C_V + ATT_WIDTH

VMEM_LIMIT_BYTES = 52 * 1024 * 1024
SUBLANES = 8
LANES = 128


def _cparams(sem):
    return pltpu.CompilerParams(dimension_semantics=sem, vmem_limit_bytes=VMEM_LIMIT_BYTES)


def _const_spec(shape):
    nd = len(shape)
    return pl.BlockSpec(shape, lambda *_: (0,) * nd)


def _layer_norm(x, g, b):
    mu = jnp.mean(x, axis=-1, keepdims=True)
    xc = x - mu
    var = jnp.mean(xc * xc, axis=-1, keepdims=True)
    return xc * lax.rsqrt(var + LN_EPS) * g + b


def _lambda_init(layer_idx):
    return 0.8 - 0.6 * math.exp(-0.3 * layer_idx)


def _diff_lambda(lamp, lam_init):
    a = jnp.sum(lamp[0:1, :] * lamp[1:2, :], axis=-1, keepdims=True)
    b = jnp.sum(lamp[2:3, :] * lamp[3:4, :], axis=-1, keepdims=True)
    return jnp.exp(a) - jnp.exp(b) + lam_init


def _store_heads(ref, val):
    n = val.shape[0]
    for h in range(N_HEADS):
        ref[pl.ds(h, n, stride=N_HEADS), :] = val[:, h * V_DIM:(h + 1) * V_DIM]


def _proj_rows_kernel(x_ref, w_ref, u_ref, q_ref, kf_ref, kb_ref, vf_ref, vb_ref):
    x = x_ref[...].astype(BF16)
    u_ref[...] = jnp.dot(x, w_ref[:, 0:C_Q], preferred_element_type=F32)
    q = jnp.dot(x, w_ref[:, C_Q:C_K], preferred_element_type=F32)
    q_ref[...] = (q * QK_SCALE).astype(BF16)
    k = jnp.dot(x, w_ref[:, C_K:C_V], preferred_element_type=F32)
    kb_ref[...] = k.astype(BF16)
    _store_heads(kf_ref, k)
    v = jnp.dot(x, w_ref[:, C_V:C_G], preferred_element_type=F32)
    vb_ref[...] = v.astype(BF16)
    _store_heads(vf_ref, v)


def _project_rows(x2d, w_bf16, tm):
    t = x2d.shape[0]
    row = lambda n: pl.BlockSpec((tm, n), lambda i: (i, 0))
    heads = pl.BlockSpec((tm * N_HEADS, V_DIM), lambda i: (i, 0))
    return pl.pallas_call(
        _proj_rows_kernel,
        grid=(t // tm,),
        in_specs=[row(D_MODEL), _const_spec((D_MODEL, C_G))],
        out_specs=[row(SSM_WIDTH), row(ATT_WIDTH), heads, row(ATT_WIDTH), heads, row(ATT_WIDTH)],
        out_shape=[
            jax.ShapeDtypeStruct((t, SSM_WIDTH), F32),
            jax.ShapeDtypeStruct((t, ATT_WIDTH), BF16),
            jax.ShapeDtypeStruct((t * N_HEADS, V_DIM), F32),
            jax.ShapeDtypeStruct((t, ATT_WIDTH), BF16),
            jax.ShapeDtypeStruct((t * N_HEADS, V_DIM), F32),
            jax.ShapeDtypeStruct((t, ATT_WIDTH), BF16),
        ],
        compiler_params=_cparams(("parallel",)),
        name="proj_rows",
    )(x2d, w_bf16)


def _proj_prompt_kernel(qblk, vblk, x_ref, w_ref, wt_ref, u_ref, kf_ref, kb_ref, vf_ref, qt_ref, vt_ref, kn_ref):
    x = x_ref[...].astype(BF16)
    u_ref[...] = jnp.dot(x, w_ref[:, 0:C_Q], preferred_element_type=F32)
    k = jnp.dot(x, w_ref[:, C_Q:C_Q + ATT_WIDTH], preferred_element_type=F32)
    kb = k.astype(BF16)
    kb_ref[...] = kb
    kr = kb.astype(F32)
    member = (lax.broadcasted_iota(jnp.int32, (ATT_WIDTH, LANES), 0) >> 6) == lax.broadcasted_iota(
        jnp.int32, (ATT_WIDTH, LANES), 1)
    ksq = jnp.dot((kr * kr).astype(BF16), jnp.where(member, 1.0, 0.0).astype(BF16), preferred_element_type=F32)
    kn_ref[0] = jnp.broadcast_to(jnp.max(ksq, axis=0, keepdims=True), (SUBLANES, LANES))
    _store_heads(kf_ref, k)
    _store_heads(vf_ref, jnp.dot(x, w_ref[:, C_Q + ATT_WIDTH:], preferred_element_type=F32))
    nt = (((1,), (1,)), ((), ()))
    qt = lax.dot_general(wt_ref[0:ATT_WIDTH, :], x, nt, preferred_element_type=F32)
    qt = (qt * (QK_SCALE * LOG2E)).astype(BF16)
    vt = lax.dot_general(wt_ref[ATT_WIDTH:, :], x, nt, preferred_element_type=F32).astype(BF16)
    for j in range(x.shape[0] // qblk):
        qt_ref[j] = qt[:, j * qblk:(j + 1) * qblk]
    for j in range(x.shape[0] // vblk):
        vt_ref[j] = vt[:, j * vblk:(j + 1) * vblk]


def _project_prompt(x2d, w_rows, w_cols, tm, qblk, vblk):
    t = x2d.shape[0]
    row = lambda n: pl.BlockSpec((tm, n), lambda i: (i, 0))
    heads = pl.BlockSpec((tm * N_HEADS, V_DIM), lambda i: (i, 0))
    tblk = lambda n: pl.BlockSpec((tm // n, ATT_WIDTH, n), lambda i: (i, 0, 0))
    return pl.pallas_call(
        functools.partial(_proj_prompt_kernel, qblk, vblk),
        grid=(t // tm,),
        in_specs=[row(D_MODEL), _const_spec(w_rows.shape), _const_spec(w_cols.shape)],
        out_specs=[row(SSM_WIDTH), heads, row(ATT_WIDTH), heads, tblk(qblk), tblk(vblk),
                   pl.BlockSpec((1, SUBLANES, LANES), lambda i: (i, 0, 0))],
        out_shape=[
            jax.ShapeDtypeStruct((t, SSM_WIDTH), F32),
            jax.ShapeDtypeStruct((t * N_HEADS, V_DIM), F32),
            jax.ShapeDtypeStruct((t, ATT_WIDTH), BF16),
            jax.ShapeDtypeStruct((t * N_HEADS, V_DIM), F32),
            jax.ShapeDtypeStruct((t // qblk, ATT_WIDTH, qblk), BF16),
            jax.ShapeDtypeStruct((t // vblk, ATT_WIDTH, vblk), BF16),
            jax.ShapeDtypeStruct((t // tm, SUBLANES, LANES), F32),
        ],
        compiler_params=_cparams(("parallel",)),
        name="proj_prompt",
    )(x2d, w_rows, w_cols)


def _ssm_tables(lam_re, lam_im, log_dt, b_re, b_im, c_re, c_im):
    dt = jnp.exp(log_dt)[:, None]
    mag = jnp.exp(lam_re * dt)
    a_re, a_im = mag * jnp.cos(lam_im * dt), mag * jnp.sin(lam_im * dt)
    den = lam_re * lam_re + lam_im * lam_im
    zr, zi = a_re - 1.0, a_im
    fr = (zr * lam_re + zi * lam_im) / den
    fi = (zi * lam_re - zr * lam_im) / den
    bb_re = fr[..., None] * b_re - fi[..., None] * b_im
    bb_im = fr[..., None] * b_im + fi[..., None] * b_re
    eye = jnp.eye(SSM_GROUPS, dtype=F32)
    blk_b = lambda m: jnp.einsum('gpc,gh->gchp', m, eye).reshape(SSM_WIDTH, N_STATE).astype(BF16)
    blk_c = lambda m: jnp.einsum('gcp,gh->gphc', m, eye).reshape(N_STATE, SSM_WIDTH).astype(BF16)
    ar, ai = a_re.reshape(1, N_STATE), a_im.reshape(1, N_STATE)

    def cmul(xr, xi, yr, yi):
        return xr * yr - xi * yi, xr * yi + xi * yr

    rows = lax.broadcasted_iota(jnp.int32, (SUBLANES, 1), 0)
    tabs = []
    pr, pi = ar, ai
    for k in range(3):
        keep = rows >= (1 << k)
        tabs += [jnp.where(keep, pr, 0.0), jnp.where(keep, pi, 0.0)]
        pr, pi = cmul(pr, pi, pr, pi)
    cr, ci = [ar], [ai]
    for _ in range(SUBLANES - 1):
        nr, ni = cmul(cr[-1], ci[-1], ar, ai)
        cr.append(nr)
        ci.append(ni)
    tabs += [jnp.concatenate(cr, axis=0), jnp.concatenate(ci, axis=0)]
    return blk_b(bb_re), blk_b(bb_im), blk_c(c_re), blk_c(-c_im), jnp.stack(tabs)


SCAN_LANES = 512


def _scan_rows(sre, sim, mult_ref, r0, carry_re, carry_im):
    out_re, out_im = [], []
    for c0 in range(0, N_STATE, SCAN_LANES):
        ls = slice(c0, c0 + SCAN_LANES)
        xr = sre[pl.ds(r0, SUBLANES), ls]
        xi = sim[pl.ds(r0, SUBLANES), ls]
        for k in range(3):
            d = 1 << k
            yr = pltpu.roll(xr, d, 0)
            yi = pltpu.roll(xi, d, 0)
            mr = mult_ref[2 * k, :, ls]
            mi = mult_ref[2 * k + 1, :, ls]
            xr, xi = xr + (mr * yr - mi * yi), xi + (mr * yi + mi * yr)
        pr = mult_ref[6, :, ls]
        pi = mult_ref[7, :, ls]
        cr = carry_re[:, ls]
        ci = carry_im[:, ls]
        xr, xi = xr + (pr * cr - pi * ci), xi + (pr * ci + pi * cr)
        sre[pl.ds(r0, SUBLANES), ls] = xr
        sim[pl.ds(r0, SUBLANES), ls] = xi
        out_re.append(xr[SUBLANES - 1:SUBLANES, :])
        out_im.append(xi[SUBLANES - 1:SUBLANES, :])
    return jnp.concatenate(out_re, axis=1), jnp.concatenate(out_im, axis=1)


def _gelu_tanh(x):
    return 0.5 * x * (1.0 + jnp.tanh(math.sqrt(2.0 / math.pi) * (x + 0.044715 * (x * x * x))))


SSM_HALVES = 2


def _ssm_halves():
    cw, sw = SSM_WIDTH // SSM_HALVES, N_STATE // SSM_HALVES
    return [(slice(i * cw, (i + 1) * cw), slice(i * sw, (i + 1) * sw)) for i in range(SSM_HALVES)]


def _ssm_head(u_ref, bre_ref, bim_ref, sre, sim):
    ub = u_ref[...].astype(BF16)
    for cs, ss in _ssm_halves():
        sre[:, ss] = jnp.dot(ub[:, cs], bre_ref[cs, ss], preferred_element_type=F32)
        sim[:, ss] = jnp.dot(ub[:, cs], bim_ref[cs, ss], preferred_element_type=F32)


def _ssm_tail(u_ref, cre_ref, ncim_ref, d_ref, wglu_ref, bglu_ref, y_ref, sre, sim):
    y = jnp.concatenate(
        [jnp.dot(sre[:, ss].astype(BF16), cre_ref[ss, cs], preferred_element_type=F32)
         + jnp.dot(sim[:, ss].astype(BF16), ncim_ref[ss, cs], preferred_element_type=F32)
         for cs, ss in _ssm_halves()], axis=1) + d_ref[...] * u_ref[...]
    y = _gelu_tanh(y)
    z = jnp.dot(y.astype(BF16), wglu_ref[...], preferred_element_type=F32) + bglu_ref[...]
    y_ref[...] = (y * jax.nn.sigmoid(z)).astype(BF16)


def _ssm_seq_kernel(u_ref, bre_ref, bim_ref, cre_ref, ncim_ref, d_ref, wglu_ref, bglu_ref, mult_ref,
                    y_ref, ore_ref, oim_ref, sre, sim, car_re, car_im):
    @pl.when(pl.program_id(1) == 0)
    def _():
        car_re[...] = jnp.zeros_like(car_re)
        car_im[...] = jnp.zeros_like(car_im)

    _ssm_head(u_ref, bre_ref, bim_ref, sre, sim)
    n_blk = u_ref.shape[0] // SUBLANES

    def body(n, carry):
        r0 = pl.multiple_of(n * SUBLANES, SUBLANES)
        return _scan_rows(sre, sim, mult_ref, r0, carry[0], carry[1])

    cr, ci = lax.fori_loop(0, n_blk, body, (car_re[...], car_im[...]))
    car_re[...] = cr
    car_im[...] = ci
    ore_ref[0] = cr
    oim_ref[0] = ci
    _ssm_tail(u_ref, cre_ref, ncim_ref, d_ref, wglu_ref, bglu_ref, y_ref, sre, sim)


def _ssm_dec_kernel(u_ref, s0re_ref, s0im_ref, bre_ref, bim_ref, cre_ref, ncim_ref, d_ref, wglu_ref, bglu_ref,
                    mult_ref, y_ref, ore_ref, oim_ref, sre, sim):
    _ssm_head(u_ref, bre_ref, bim_ref, sre, sim)
    n_blk = u_ref.shape[0] // SUBLANES

    def body(n, _):
        r0 = pl.multiple_of(n * SUBLANES, SUBLANES)
        cr, ci = _scan_rows(sre, sim, mult_ref, r0, s0re_ref[pl.ds(n, 1), :], s0im_ref[pl.ds(n, 1), :])
        ore_ref[pl.ds(n, 1), :] = cr
        oim_ref[pl.ds(n, 1), :] = ci
        return 0

    lax.fori_loop(0, n_blk, body, 0)
    _ssm_tail(u_ref, cre_ref, ncim_ref, d_ref, wglu_ref, bglu_ref, y_ref, sre, sim)


def _ssm_weight_specs():
    return [_const_spec((SSM_WIDTH, N_STATE)), _const_spec((SSM_WIDTH, N_STATE)),
            _const_spec((N_STATE, SSM_WIDTH)), _const_spec((N_STATE, SSM_WIDTH)),
            _const_spec((1, SSM_WIDTH)), _const_spec((SSM_WIDTH, SSM_WIDTH)), _const_spec((1, SSM_WIDTH)),
            _const_spec((8, SUBLANES, N_STATE))]


def _ssm_prompt(u3, wts, lc):
    bsz, seq, _ = u3.shape
    return pl.pallas_call(
        _ssm_seq_kernel,
        grid=(bsz, seq // lc),
        in_specs=[pl.BlockSpec((None, lc, SSM_WIDTH), lambda b, c: (b, c, 0))] + _ssm_weight_specs(),
        out_specs=[pl.BlockSpec((None, lc, SSM_WIDTH), lambda b, c: (b, c, 0)),
                   pl.BlockSpec((1, 1, N_STATE), lambda b, c: (b, 0, 0)),
                   pl.BlockSpec((1, 1, N_STATE), lambda b, c: (b, 0, 0))],
        out_shape=[jax.ShapeDtypeStruct((bsz, seq, SSM_WIDTH), BF16),
                   jax.ShapeDtypeStruct((bsz, 1, N_STATE), F32),
                   jax.ShapeDtypeStruct((bsz, 1, N_STATE), F32)],
        scratch_shapes=[pltpu.VMEM((lc, N_STATE), F32), pltpu.VMEM((lc, N_STATE), F32),
                        pltpu.VMEM((1, N_STATE), F32), pltpu.VMEM((1, N_STATE), F32)],
        compiler_params=_cparams(("parallel", "arbitrary")),
        name="ssm_prompt",
    )(u3, *wts)


def _ssm_sample(u2, s0_re, s0_im, wts, lc):
    t = u2.shape[0]
    nseq = lc // SUBLANES
    return pl.pallas_call(
        _ssm_dec_kernel,
        grid=(t // lc,),
        in_specs=[pl.BlockSpec((lc, SSM_WIDTH), lambda c: (c, 0)),
                  pl.BlockSpec((nseq, N_STATE), lambda c: (c, 0)),
                  pl.BlockSpec((nseq, N_STATE), lambda c: (c, 0))] + _ssm_weight_specs(),
        out_specs=[pl.BlockSpec((lc, SSM_WIDTH), lambda c: (c, 0)),
                   pl.BlockSpec((nseq, N_STATE), lambda c: (c, 0)),
                   pl.BlockSpec((nseq, N_STATE), lambda c: (c, 0))],
        out_shape=[jax.ShapeDtypeStruct((t, SSM_WIDTH), BF16),
                   jax.ShapeDtypeStruct((t // SUBLANES, N_STATE), F32),
                   jax.ShapeDtypeStruct((t // SUBLANES, N_STATE), F32)],
        scratch_shapes=[pltpu.VMEM((lc, N_STATE), F32), pltpu.VMEM((lc, N_STATE), F32)],
        compiler_params=_cparams(("parallel",)),
        name="ssm_sample",
    )(u2, s0_re, s0_im, *wts)


def _head_rms(o, g, lam_init):
    ms = jnp.mean(o * o, axis=-1, keepdims=True)
    return o * lax.rsqrt(ms + LN_EPS) * g * (1.0 - lam_init)


N_BIAS_TERMS = 3
ATTN_BQ = 256
ATTN_BK = 256
ATTN_HEADS = 2
ATTN_UNROLL = 4
UNDERFLOW_BITS = 160.0
NORM_SLACK = 1.05
ONES_ROWS = 16


def _attn_kernel(lam_init, bq, bk, nh, slopes_ref, lamp_ref, g_ref, qt_ref, k_ref, vt_ref, kn_ref, o_ref,
                 acc_ref, m_ref, a_ref, u_ref, p_ref):
    hg = pl.program_id(1)
    qi = pl.program_id(2)
    two = 2 * bq
    n_diag = bq // bk
    heads = range(nh)
    hs = lambda hh: slice(hh * V_DIM, (hh + 1) * V_DIM)

    slope2, w, bias_cols = [], [], []
    r = lax.broadcasted_iota(jnp.int32, (V_DIM, bq), 0)
    ones_rows = lax.broadcasted_iota(jnp.int32, (V_DIM, two), 0) < N_BIAS_TERMS
    c = lax.broadcasted_iota(jnp.int32, (bk, V_DIM), 0).astype(F32)
    lane = lax.broadcasted_iota(jnp.int32, (bk, V_DIM), 1)
    for hh in heads:
        slope2.append(slopes_ref[hg * nh + hh] * LOG2E)
        qt = qt_ref[0, hs(hh), :]
        zero = jnp.zeros_like(qt)
        w_top = jnp.concatenate([jnp.where(r < HEAD_DIM, qt, zero), jnp.where(r >= HEAD_DIM, qt, zero)], axis=1)
        w.append(jnp.concatenate([w_top, jnp.where(ones_rows, 1.0, 0.0).astype(BF16)], axis=0))
        rem = slope2[hh] * c
        cols = jnp.zeros((bk, V_DIM), F32)
        for t in range(N_BIAS_TERMS):
            term = rem.astype(BF16).astype(F32)
            cols = jnp.where(lane == t, term, cols)
            rem = rem - term
        bias_cols.append(cols.astype(BF16))

    ksq_all = jnp.max(jnp.max(kn_ref[...], axis=0), axis=0, keepdims=True)
    klane = lax.broadcasted_iota(jnp.int32, (1, LANES), 1) >> 1
    first_f = None
    for hh in heads:
        kn = jnp.sqrt(jnp.max(jnp.where(klane == hg * nh + hh, ksq_all, 0.0), axis=1, keepdims=True))
        qf = qt_ref[0, hs(hh), :].astype(F32)
        q2 = qf * qf
        qsq = jnp.maximum(jnp.sum(q2[:HEAD_DIM], axis=0, keepdims=True), jnp.sum(q2[HEAD_DIM:], axis=0, keepdims=True))
        qn = jnp.sqrt(jnp.max(qsq, axis=1, keepdims=True))
        reach = (NORM_SLACK * 2.0 * qn * kn + UNDERFLOW_BITS) / slope2[hh] + bk
        f = jnp.floor((jnp.full((1, 1), qi * bq, jnp.int32).astype(F32) - reach) * (1.0 / bk)) + 1.0
        first_f = f if first_f is None else jnp.minimum(first_f, f)
    n_full = qi * n_diag
    first = jnp.clip(first_f, 0.0, 1e9).astype(jnp.int32)[0, 0]
    first = jnp.minimum(first & -2, n_full & -2)

    ones_rows16 = jnp.ones((ONES_ROWS, bk), BF16)

    m_ref[...] = jnp.full(m_ref.shape, NEG_INF, F32)
    a_ref[...] = jnp.ones(a_ref.shape, F32)
    acc_ref[...] = jnp.zeros(acc_ref.shape, F32)
    for hh in heads:
        p_ref[2 * hh + 1] = jnp.zeros((bk, two), BF16)

    def scores(hh, kj, slot):
        ks = pl.multiple_of(kj * bk, bk)
        kaug = jnp.concatenate([k_ref[pl.ds(ks, bk), hs(hh)], bias_cols[hh]], axis=1)
        u_ref[2 * hh + slot] = jnp.dot(kaug, w[hh], preferred_element_type=F32)

    def softmax(hh, kj, slot, diag):
        u = u_ref[2 * hh + slot]
        if diag is not None:
            keep = (lax.broadcasted_iota(jnp.int32, (bk, two), 1) & (bq - 1)) >= (
                lax.broadcasted_iota(jnp.int32, (bk, two), 0) + diag * bk)
            u = jnp.where(keep, u, NEG_INF)
        off = jnp.full((1, 1), kj * bk - qi * bq, jnp.int32).astype(F32) * slope2[hh]
        m_old = m_ref[hh]
        m_new = jnp.maximum(m_old, jnp.max(u, axis=0, keepdims=True) + off)
        a_ref[hh] = jnp.exp2(m_old - m_new)
        m_ref[hh] = m_new
        p_ref[2 * hh + slot] = jnp.exp2(u + (off - m_new)).astype(BF16)

    def values(hh, kj, slot):
        vaug = jnp.concatenate([vt_ref[jnp.maximum(kj, 0), hs(hh), :], ones_rows16], axis=0)
        acc_ref[hh] = a_ref[hh] * acc_ref[hh] + jnp.dot(vaug, p_ref[2 * hh + slot], preferred_element_type=F32)

    def run(kj, slot):
        for hh in heads:
            values(hh, kj - 1, 1 - slot)
            softmax(hh, kj, slot, None)
            scores(hh, kj + 1, 1 - slot)

    for hh in heads:
        scores(hh, first, 0)

    def body(i, carry):
        for d in range(ATTN_UNROLL):
            run(first + i * ATTN_UNROLL + d, d & 1)
        return carry

    n_run = n_full - first
    n_main = n_run // ATTN_UNROLL
    lax.fori_loop(0, n_main, body, 0)
    base = first + n_main * ATTN_UNROLL
    arm = ATTN_UNROLL // 2
    while arm >= 1:
        if arm > 1 or n_diag % 2 != 0:
            @pl.when((n_run & arm) != 0)
            def _(base=base, arm=arm):
                for d in range(arm):
                    run(base + d, d & 1)
            base = base + (n_run & arm)
        arm //= 2
    for d in range(n_diag):
        slot = d & 1 if n_diag % 2 == 0 else (n_full + d) & 1
        for hh in heads:
            values(hh, n_full + d - 1, 1 - slot)
            softmax(hh, n_full + d, slot, d)
            if d + 1 < n_diag:
                scores(hh, n_full + d + 1, 1 - slot)
    lam = _diff_lambda(lamp_ref[...], lam_init)
    for hh in heads:
        values(hh, n_full + n_diag - 1, slot)
        acc = acc_ref[hh]
        inv = 1.0 / acc[V_DIM:V_DIM + 1, :]
        acc = acc[:V_DIM, :]
        ot = acc[:, :bq] * inv[:, :bq] - lam * (acc[:, bq:] * inv[:, bq:])
        ms = jnp.mean(ot * ot, axis=0, keepdims=True)
        ot = ot * lax.rsqrt(ms + LN_EPS) * (g_ref[...] * (1.0 - lam_init))
        o_ref[0, :, hs(hh)] = ot.T.astype(o_ref.dtype)


def _attn_prompt(qt, k, vt, kn, slopes, lamp, g_col, lam_init, bsz, seq, bq, bk, nh):
    nq, nk = seq // bq, seq // bk
    hw = nh * V_DIM
    kn_tiles = kn.shape[0] // bsz
    return pl.pallas_call(
        functools.partial(_attn_kernel, lam_init, bq, bk, nh),
        grid=(bsz, N_HEADS // nh, nq),
        in_specs=[pl.BlockSpec(memory_space=pltpu.SMEM),
                  _const_spec((4, HEAD_DIM)), _const_spec((V_DIM, 1)),
                  pl.BlockSpec((1, hw, bq), lambda b, h, i: (b * nq + i, h, 0)),
                  pl.BlockSpec((seq, hw), lambda b, h, i: (b, h)),
                  pl.BlockSpec((nk, hw, bk), lambda b, h, i: (b, h, 0)),
                  pl.BlockSpec((kn_tiles, SUBLANES, LANES), lambda b, h, i: (b, 0, 0))],
        out_specs=pl.BlockSpec((1, bq, hw), lambda b, h, i: (b, i, h)),
        out_shape=jax.ShapeDtypeStruct((bsz, seq, ATT_WIDTH), BF16),
        scratch_shapes=[pltpu.VMEM((nh, V_DIM + ONES_ROWS, 2 * bq), F32),
                        pltpu.VMEM((nh, 1, 2 * bq), F32), pltpu.VMEM((nh, 1, 2 * bq), F32),
                        pltpu.VMEM((2 * nh, bk, 2 * bq), F32), pltpu.VMEM((2 * nh, bk, 2 * bq), BF16)],
        compiler_params=_cparams(("parallel", "parallel", "arbitrary")),
        name="attn_prompt",
    )(slopes, lamp, g_col, qt, k, vt, kn)


def _page_matrix(ref):
    return jnp.concatenate([ref[0, pl.ds(h, PAGE_SIZE, stride=N_HEADS), :].astype(BF16) for h in range(N_HEADS)],
                           axis=1)


def _sattn_kernel(lam_init, n_pages, pt_ref, lamp_ref, g_ref, q_ref, kn_ref, vn_ref, *rest):
    del pt_ref
    k_refs = rest[:n_pages]
    v_refs = rest[n_pages:2 * n_pages]
    o_ref = rest[2 * n_pages]
    s_scr = rest[2 * n_pages + 1]
    t_new = q_ref.shape[1]
    n_past = n_pages * PAGE_SIZE
    nl = 2 * N_HEADS * t_new
    nt = (((1,), (1,)), ((), ()))
    tn = (((0,), (0,)), ((), ()))

    c = lax.broadcasted_iota(jnp.int32, (1, nl), 1)
    c_q = c & (t_new - 1)
    c_h = (c >> 3) & (N_HEADS - 1)
    c_m = c >> 6
    slope = jnp.exp2(-(c_h + 1).astype(F32))
    qpos = (n_past + c_q).astype(F32)

    r = lax.broadcasted_iota(jnp.int32, (nl, t_new), 0)
    sel = (r & (t_new - 1)) == lax.broadcasted_iota(jnp.int32, (nl, t_new), 1)
    rep = jnp.dot(sel.astype(BF16), q_ref[0], preferred_element_type=F32)
    rr = lax.broadcasted_iota(jnp.int32, (nl, ATT_WIDTH), 0)
    jj = lax.broadcasted_iota(jnp.int32, (nl, ATT_WIDTH), 1)
    same_head = ((rr >> 3) & (N_HEADS - 1)) == (jj >> 7)
    keep = same_head & ((rr >> 6) == ((jj >> 6) & 1))
    qbd = jnp.where(keep, rep, 0.0).astype(BF16)

    row = lax.broadcasted_iota(jnp.int32, (PAGE_SIZE, 1), 0).astype(F32)
    m = jnp.full((1, nl), NEG_INF, F32)
    for i in range(n_pages):
        s = lax.dot_general(_page_matrix(k_refs[i]), qbd, nt, preferred_element_type=F32)
        s = s - slope * (qpos - (row + float(i * PAGE_SIZE)))
        s_scr[i * PAGE_SIZE:(i + 1) * PAGE_SIZE, :] = s
        m = jnp.maximum(m, jnp.max(s, axis=0, keepdims=True))
    dist = c_q - lax.broadcasted_iota(jnp.int32, (t_new, 1), 0)
    sn = lax.dot_general(kn_ref[0].astype(BF16), qbd, nt, preferred_element_type=F32)
    sn = jnp.where(dist >= 0, sn - slope * dist.astype(F32), NEG_INF)
    m = jnp.maximum(m, jnp.max(sn, axis=0, keepdims=True))

    l = jnp.zeros((1, nl), F32)
    for i in range(n_pages):
        p = jnp.exp(s_scr[i * PAGE_SIZE:(i + 1) * PAGE_SIZE, :] - m)
        s_scr[i * PAGE_SIZE:(i + 1) * PAGE_SIZE, :] = p
        l = l + jnp.sum(p, axis=0, keepdims=True)
    pn = jnp.exp(sn - m)
    l = l + jnp.sum(pn, axis=0, keepdims=True)

    lam = _diff_lambda(lamp_ref[...], lam_init)
    coef = jnp.where(c_m == 0, 1.0, -lam) / l
    acc = lax.dot_general((pn * coef).astype(BF16), vn_ref[0].astype(BF16), tn, preferred_element_type=F32)
    for i in range(n_pages):
        w = (s_scr[i * PAGE_SIZE:(i + 1) * PAGE_SIZE, :] * coef).astype(BF16)
        acc = acc + lax.dot_general(w, _page_matrix(v_refs[i]), tn, preferred_element_type=F32)
    acc = jnp.where(same_head, acc, 0.0)
    o = jnp.sum(acc.reshape(nl // t_new, t_new, ATT_WIDTH), axis=0)
    for hh in range(N_HEADS):
        ls = slice(hh * V_DIM, (hh + 1) * V_DIM)
        o_ref[0, :, ls] = _head_rms(o[:, ls], g_ref[...], lam_init).astype(o_ref.dtype)


def _attn_sample(q, kn, vn, cache_k, cache_v, page_table, lamp, g, lam_init):
    nseq, t_new, _ = q.shape
    n_pages = page_table.shape[1]
    tok = pl.BlockSpec((1, t_new, ATT_WIDTH), lambda s, pt: (s, 0, 0))
    page = lambda i: pl.BlockSpec((1, PAGE_SIZE * N_HEADS, V_DIM), lambda s, pt: (pt[s, i], 0, 0))
    grid_spec = pltpu.PrefetchScalarGridSpec(
        num_scalar_prefetch=1,
        grid=(nseq,),
        in_specs=[pl.BlockSpec((4, HEAD_DIM), lambda s, pt: (0, 0)), pl.BlockSpec((1, V_DIM), lambda s, pt: (0, 0)),
                  tok, tok, tok]
                 + [page(i) for i in range(n_pages)] + [page(i) for i in range(n_pages)],
        out_specs=tok,
        scratch_shapes=[pltpu.VMEM((n_pages * PAGE_SIZE, 2 * N_HEADS * t_new), F32)],
    )
    return pl.pallas_call(
        functools.partial(_sattn_kernel, lam_init, n_pages),
        grid_spec=grid_spec,
        out_shape=jax.ShapeDtypeStruct((nseq, t_new, ATT_WIDTH), BF16),
        compiler_params=_cparams(("arbitrary",)),
        name="attn_sample",
    )(page_table, lamp, g, q, kn, vn, *([cache_k] * n_pages), *([cache_v] * n_pages))


def _merge_kernel(x_ref, ssm_ref, att_ref, wg_ref, bg_ref, pssm_ref, patt_ref, wo_ref, g_ref, b_ref, o_ref):
    x = x_ref[...]
    gates = jax.nn.sigmoid(jnp.dot(x.astype(BF16), wg_ref[...], preferred_element_type=F32) + bg_ref[...])
    merged = (gates[:, :D_MODEL] * jnp.dot(ssm_ref[...], pssm_ref[...], preferred_element_type=F32)
              + gates[:, D_MODEL:] * jnp.dot(att_ref[...], patt_ref[...], preferred_element_type=F32))
    y = DN_ALPHA * x + jnp.dot(merged.astype(BF16), wo_ref[...], preferred_element_type=F32)
    o_ref[...] = _layer_norm(y, g_ref[...], b_ref[...])


def _merge(x2d, ssm_y, att, wg, bg, pssm, patt, wo, g, b, tm):
    t = x2d.shape[0]
    row = lambda n: pl.BlockSpec((tm, n), lambda i: (i, 0))
    return pl.pallas_call(
        _merge_kernel,
        grid=(t // tm,),
        in_specs=[row(D_MODEL), row(SSM_WIDTH), row(ATT_WIDTH),
                  _const_spec((D_MODEL, 2 * D_MODEL)), _const_spec((1, 2 * D_MODEL)),
                  _const_spec((SSM_WIDTH, D_MODEL)), _const_spec((ATT_WIDTH, D_MODEL)),
                  _const_spec((D_MODEL, D_MODEL)), _const_spec((1, D_MODEL)), _const_spec((1, D_MODEL))],
        out_specs=row(D_MODEL),
        out_shape=jax.ShapeDtypeStruct((t, D_MODEL), F32),
        compiler_params=_cparams(("parallel",)),
        name="merge",
    )(x2d, ssm_y, att, wg, bg, pssm, patt, wo, g, b)


def _route(x_bf16, wrg, brg, wre, bre):
    gl = jnp.dot(x_bf16, wrg, preferred_element_type=F32) + brg
    el = jnp.dot(x_bf16, wre, preferred_element_type=F32) + bre
    gi = lax.broadcasted_iota(jnp.int32, gl.shape, 1).astype(F32)
    gmax = jnp.max(gl, axis=-1, keepdims=True)
    g_sel = jnp.min(jnp.where(gl == gmax, gi, float(N_EXPERT_GROUPS)), axis=-1, keepdims=True)
    p_g = 1.0 / jnp.sum(jnp.exp(gl - gmax), axis=-1, keepdims=True)
    ei_int = lax.broadcasted_iota(jnp.int32, el.shape, 1)
    ei = ei_int.astype(F32)
    el = jnp.where((ei_int >> 3).astype(F32) == g_sel, el, -jnp.inf)
    v1 = jnp.max(el, axis=-1, keepdims=True)
    i1 = jnp.min(jnp.where(el == v1, ei, float(N_EXPERTS)), axis=-1, keepdims=True)
    el2 = jnp.where(ei == i1, -jnp.inf, el)
    v2 = jnp.max(el2, axis=-1, keepdims=True)
    i2 = jnp.min(jnp.where(el2 == v2, ei, float(N_EXPERTS)), axis=-1, keepdims=True)
    e2 = jnp.exp(v2 - v1)
    w1 = p_g / (1.0 + e2)
    w2 = p_g * e2 / (1.0 + e2)
    lane = lax.broadcasted_iota(jnp.int32, (gl.shape[0], LANES), 1).astype(F32)
    first = EXPERTS_PER_GROUP * g_sel
    comb = jnp.where(lane == i1 - first, w1, 0.0) + jnp.where(lane == i2 - first, w2, 0.0)
    return jnp.where(lane == g_sel, 1.0, 0.0), comb


MOE_ROWS = 128
MOE_EXPERTS_PER_STEP = 4


def _moe_kernel(x_ref, wrg_ref, brg_ref, wre_ref, bre_ref, wg_ref, wu_ref, wd_ref, g_ref, b_ref, o_ref,
                xs_scr, comb_scr, acc_scr, unsort_scr, seg_ref):
    e = pl.program_id(1)
    tm = x_ref.shape[0]
    srows = xs_scr.shape[0]
    lane = lax.broadcasted_iota(jnp.int32, (1, LANES), 1)

    @pl.when(e == 0)
    def _():
        xb = x_ref[...].astype(BF16)
        group_1h, comb = _route(xb, wrg_ref[...], brg_ref[...], wre_ref[...], bre_ref[...])
        tri = lax.broadcasted_iota(jnp.int32, (tm, tm), 0) >= lax.broadcasted_iota(jnp.int32, (tm, tm), 1)
        count = jnp.dot(jnp.where(tri, 1.0, 0.0).astype(BF16), group_1h.astype(BF16), preferred_element_type=F32)
        total = count[tm - 1:tm, :]
        padded = jnp.floor((total + (MOE_ROWS - 1)) * (1.0 / MOE_ROWS)) * MOE_ROWS
        start = jnp.zeros((1, LANES), F32)
        for gidx in range(N_EXPERT_GROUPS):
            size = jnp.sum(jnp.where(lane == gidx, padded, 0.0), axis=-1, keepdims=True)
            seg_ref[0, gidx] = jnp.sum(jnp.where(lane == gidx, start, 0.0)).astype(jnp.int32) // MOE_ROWS
            seg_ref[1, gidx] = jnp.sum(size).astype(jnp.int32) // MOE_ROWS
            start = start + jnp.where(lane > gidx, size, 0.0)
        pos = jnp.sum(group_1h * (start + count - 1.0), axis=-1, keepdims=True)
        unsort = jnp.where(lax.broadcasted_iota(jnp.int32, (tm, srows), 1).astype(F32) == pos, 1.0, 0.0)
        unsort_scr[...] = unsort.astype(BF16)
        pos_row = jnp.broadcast_to(pos, (tm, LANES)).T[0:1, :]
        sort = jnp.where(lax.broadcasted_iota(jnp.int32, (srows, tm), 0).astype(F32) == pos_row, 1.0, 0.0)
        sort = sort.astype(BF16)
        xs_scr[...] = jnp.dot(sort, xb, preferred_element_type=F32).astype(BF16)
        comb_hi = comb.astype(BF16)
        comb_lo = (comb - comb_hi.astype(F32)).astype(BF16)
        comb_scr[...] = (jnp.dot(sort, comb_hi, preferred_element_type=F32)
                         + jnp.dot(sort, comb_lo, preferred_element_type=F32))
        acc_scr[...] = jnp.zeros_like(acc_scr)

    first_expert = e * MOE_EXPERTS_PER_STEP
    grp = first_expert >> 3
    in_grp = first_expert & (EXPERTS_PER_GROUP - 1)
    first_blk = seg_ref[0, grp]

    def block(i, carry):
        r0 = pl.multiple_of((first_blk + i) * MOE_ROWS, MOE_ROWS)
        xs = xs_scr[pl.ds(r0, MOE_ROWS), :]
        comb = comb_scr[pl.ds(r0, MOE_ROWS), :]
        out = None
        for k in range(MOE_EXPERTS_PER_STEP):
            hg = jnp.dot(xs, wg_ref[k], preferred_element_type=F32)
            hu = jnp.dot(xs, wu_ref[k], preferred_element_type=F32)
            ce = jnp.sum(jnp.where(lane == in_grp + k, comb, 0.0), axis=-1, keepdims=True)
            hid = (hg * jax.nn.sigmoid(hg)) * hu * ce
            part = jnp.dot(hid.astype(BF16), wd_ref[k], preferred_element_type=F32)
            out = part if out is None else out + part
        acc_scr[pl.ds(r0, MOE_ROWS), :] += out
        return carry

    lax.fori_loop(0, seg_ref[1, grp], block, 0)

    @pl.when(e == N_EXPERTS // MOE_EXPERTS_PER_STEP - 1)
    def _():
        ffn = jnp.dot(unsort_scr[...], acc_scr[...].astype(BF16), preferred_element_type=F32)
        o_ref[...] = _layer_norm(DN_ALPHA * x_ref[...] + ffn, g_ref[...], b_ref[...])


def _moe(x1, wrg, brg, wre, bre, wg, wu, wd, g, b, tm):
    t = x1.shape[0]
    srows = tm + N_EXPERT_GROUPS * MOE_ROWS
    row = pl.BlockSpec((tm, D_MODEL), lambda i, e: (i, 0))
    return pl.pallas_call(
        _moe_kernel,
        grid=(t // tm, N_EXPERTS // MOE_EXPERTS_PER_STEP),
        in_specs=[row,
                  _const_spec((D_MODEL, N_EXPERT_GROUPS)), _const_spec((1, N_EXPERT_GROUPS)),
                  _const_spec((D_MODEL, N_EXPERTS)), _const_spec((1, N_EXPERTS)),
                  pl.BlockSpec((MOE_EXPERTS_PER_STEP, D_MODEL, EXPERT_FF), lambda i, e: (e, 0, 0)),
                  pl.BlockSpec((MOE_EXPERTS_PER_STEP, D_MODEL, EXPERT_FF), lambda i, e: (e, 0, 0)),
                  pl.BlockSpec((MOE_EXPERTS_PER_STEP, EXPERT_FF, D_MODEL), lambda i, e: (e, 0, 0)),
                  _const_spec((1, D_MODEL)), _const_spec((1, D_MODEL))],
        out_specs=row,
        out_shape=jax.ShapeDtypeStruct((t, D_MODEL), F32),
        scratch_shapes=[pltpu.VMEM((srows, D_MODEL), BF16), pltpu.VMEM((srows, LANES), F32),
                        pltpu.VMEM((srows, D_MODEL), F32), pltpu.VMEM((tm, srows), BF16),
                        pltpu.SMEM((2, N_EXPERT_GROUPS), jnp.int32)],
        compiler_params=_cparams(("parallel", "arbitrary")),
        name="moe",
    )(x1, wrg, brg, wre, bre, wg, wu, wd, g, b)


def kernel(x_prompt, x_sample, cache_k, cache_v, state_ssm_re, state_ssm_im, page_table, w_in, b_gate, ssm_lambda_re, ssm_lambda_im, ssm_log_dt, ssm_b_re, ssm_b_im, ssm_c_re, ssm_c_im, ssm_d, w_glu, b_glu, lambda_q1, lambda_k1, lambda_q2, lambda_k2, subln_g, p_ssm, p_att, w_o, ln1_g, ln1_b, w_router_group, b_router_group, w_router_expert, b_router_expert, w_exp_gate, w_exp_up, w_exp_down, ln2_g, ln2_b):
    bsz, seq, _ = x_prompt.shape
    nseq, t_new, _ = x_sample.shape
    assert w_in.shape[0] == DEPTH == 1 and t_new == SUBLANES
    l = 0
    lam_init = _lambda_init(l)
    row = lambda a: a[l].reshape(1, -1)

    w_qkv = w_in[l][:, :C_G].astype(BF16)
    w_gate = w_in[l][:, C_G:].astype(BF16)
    w_ukv = jnp.concatenate([w_qkv[:, :C_Q], w_qkv[:, C_K:]], axis=1)
    w_qv_t = jnp.concatenate([w_qkv[:, C_Q:C_K], w_qkv[:, C_V:]], axis=1).T
    ssm_tabs = _ssm_tables(ssm_lambda_re[l], ssm_lambda_im[l], ssm_log_dt[l], ssm_b_re[l], ssm_b_im[l],
                           ssm_c_re[l], ssm_c_im[l])
    bre, bim, cre, ncim, mult = ssm_tabs
    ssm_wts = (bre, bim, cre, ncim, row(ssm_d), w_glu[l].astype(BF16), row(b_glu), mult)
    lamp = jnp.stack([lambda_q1[l], lambda_k1[l], lambda_q2[l], lambda_k2[l]])
    g_sub = row(subln_g)
    slopes = jnp.exp2(-8.0 * (jnp.arange(N_HEADS, dtype=F32) + 1.0) / N_HEADS)
    merge_wts = (w_gate, row(b_gate), p_ssm[l].astype(BF16), p_att[l].astype(BF16), w_o[l].astype(BF16),
                 row(ln1_g), row(ln1_b))
    moe_wts = (w_router_group[l].astype(BF16), row(b_router_group), w_router_expert[l].astype(BF16),
               row(b_router_expert), w_exp_gate[l].astype(BF16), w_exp_up[l].astype(BF16),
               w_exp_down[l].astype(BF16), row(ln2_g), row(ln2_b))

    tp = bsz * seq
    xp = x_prompt.reshape(tp, D_MODEL)
    bq, bk = min(ATTN_BQ, seq), min(ATTN_BK, seq)
    u, kf, kb, vf, qt, vt, kn = _project_prompt(xp, w_ukv, w_qv_t, min(512, seq), bq, bk)
    ssm_y, srp, sip = _ssm_prompt(u.reshape(bsz, seq, SSM_WIDTH), ssm_wts, min(512, seq))
    att = _attn_prompt(qt, kb, vt, kn, slopes, lamp, subln_g[l].reshape(V_DIM, 1), lam_init, bsz, seq, bq, bk,
                       ATTN_HEADS)
    x1 = _merge(xp, ssm_y.reshape(tp, SSM_WIDTH), att.reshape(tp, ATT_WIDTH), *merge_wts, min(256, tp))
    y_prompt = _moe(x1, *moe_wts, min(1024, tp)).reshape(bsz, seq, D_MODEL)
    k_prompt = kf.reshape(1, bsz, seq, N_HEADS, V_DIM)
    v_prompt = vf.reshape(1, bsz, seq, N_HEADS, V_DIM)
    srp = srp.reshape(1, bsz, SSM_GROUPS, SSM_STATE)
    sip = sip.reshape(1, bsz, SSM_GROUPS, SSM_STATE)

    ts = nseq * t_new
    xs = x_sample.reshape(ts, D_MODEL)
    u, q, kf, kb, vf, vb = _project_rows(xs, w_qkv, min(512, ts))
    ssm_y, srs, sis = _ssm_sample(u, state_ssm_re[l].reshape(nseq, N_STATE), state_ssm_im[l].reshape(nseq, N_STATE),
                                  ssm_wts, min(512, ts))
    att = _attn_sample(q.reshape(nseq, t_new, ATT_WIDTH), kb.reshape(nseq, t_new, ATT_WIDTH),
                       vb.reshape(nseq, t_new, ATT_WIDTH),
                       cache_k[l].reshape(-1, PAGE_SIZE * N_HEADS, V_DIM),
                       cache_v[l].reshape(-1, PAGE_SIZE * N_HEADS, V_DIM), page_table, lamp, g_sub, lam_init)
    x1 = _merge(xs, ssm_y, att.reshape(ts, ATT_WIDTH), *merge_wts, min(256, ts))
    y_sample = _moe(x1, *moe_wts, min(1024, ts)).reshape(nseq, t_new, D_MODEL)
    k_sample = kf.reshape(1, nseq, t_new, N_HEADS, V_DIM)
    v_sample = vf.reshape(1, nseq, t_new, N_HEADS, V_DIM)
    srs = srs.reshape(1, nseq, SSM_GROUPS, SSM_STATE)
    sis = sis.reshape(1, nseq, SSM_GROUPS, SSM_STATE)

    return (y_prompt, y_sample, k_prompt, v_prompt, srp, sip, k_sample, v_sample, srs, sis)
```

```python
import functools
import math

import jax
import jax.numpy as jnp
from jax import lax
from jax.experimental import pallas as pl
from jax.experimental.pallas import tpu as pltpu

F32 = jnp.float32
BF16 = jnp.bfloat16

D_MODEL = 1024
N_HEADS = 8
HEAD_DIM = 64
V_DIM = 2 * HEAD_DIM
ATT_WIDTH = N_HEADS * V_DIM
SSM_WIDTH = 512
SSM_GROUP = 16
SSM_GROUPS = SSM_WIDTH // SSM_GROUP
SSM_STATE = 64
N_STATE = SSM_GROUPS * SSM_STATE
N_EXPERT_GROUPS = 4
EXPERTS_PER_GROUP = 8
N_EXPERTS = N_EXPERT_GROUPS * EXPERTS_PER_GROUP
EXPERT_FF = 256
PAGE_SIZE = 128
DEPTH = 1
DN_ALPHA = (2.0 * DEPTH) ** 0.25
LN_EPS = 1e-5
NEG_INF = -1e30
QK_SCALE = HEAD_DIM ** -0.5
LOG2E = math.log2(math.e)
C_Q = SSM_WIDTH
C_K = C_Q + ATT_WIDTH
C_V = C_K + ATT_WIDTH
C_G = C_V + ATT_WIDTH

VMEM_LIMIT_BYTES = 52 * 1024 * 1024
SUBLANES = 8
LANES = 128


def _cparams(sem):
    return pltpu.CompilerParams(dimension_semantics=sem, vmem_limit_bytes=VMEM_LIMIT_BYTES)


def _const_spec(shape):
    nd = len(shape)
    return pl.BlockSpec(shape, lambda *_: (0,) * nd)


def _layer_norm(x, g, b):
    mu = jnp.mean(x, axis=-1, keepdims=True)
    xc = x - mu
    var = jnp.mean(xc * xc, axis=-1, keepdims=True)
    return xc * lax.rsqrt(var + LN_EPS) * g + b


def _lambda_init(layer_idx):
    return 0.8 - 0.6 * math.exp(-0.3 * layer_idx)


def _diff_lambda(lamp, lam_init):
    a = jnp.sum(lamp[0:1, :] * lamp[1:2, :], axis=-1, keepdims=True)
    b = jnp.sum(lamp[2:3, :] * lamp[3:4, :], axis=-1, keepdims=True)
    return jnp.exp(a) - jnp.exp(b) + lam_init


def _store_heads(ref, val):
    n = val.shape[0]
    for h in range(N_HEADS):
        ref[pl.ds(h, n, stride=N_HEADS), :] = val[:, h * V_DIM:(h + 1) * V_DIM]


def _proj_rows_kernel(x_ref, w_ref, u_ref, q_ref, kf_ref, kb_ref, vf_ref, vb_ref):
    x = x_ref[...].astype(BF16)
    u_ref[...] = jnp.dot(x, w_ref[:, 0:C_Q], preferred_element_type=F32)
    q = jnp.dot(x, w_ref[:, C_Q:C_K], preferred_element_type=F32)
    q_ref[...] = (q * QK_SCALE).astype(BF16)
    k = jnp.dot(x, w_ref[:, C_K:C_V], preferred_element_type=F32)
    kb_ref[...] = k.astype(BF16)
    _store_heads(kf_ref, k)
    v = jnp.dot(x, w_ref[:, C_V:C_G], preferred_element_type=F32)
    vb_ref[...] = v.astype(BF16)
    _store_heads(vf_ref, v)


def _project_rows(x2d, w_bf16, tm):
    t = x2d.shape[0]
    row = lambda n: pl.BlockSpec((tm, n), lambda i: (i, 0))
    heads = pl.BlockSpec((tm * N_HEADS, V_DIM), lambda i: (i, 0))
    return pl.pallas_call(
        _proj_rows_kernel,
        grid=(t // tm,),
        in_specs=[row(D_MODEL), _const_spec((D_MODEL, C_G))],
        out_specs=[row(SSM_WIDTH), row(ATT_WIDTH), heads, row(ATT_WIDTH), heads, row(ATT_WIDTH)],
        out_shape=[
            jax.ShapeDtypeStruct((t, SSM_WIDTH), F32),
            jax.ShapeDtypeStruct((t, ATT_WIDTH), BF16),
            jax.ShapeDtypeStruct((t * N_HEADS, V_DIM), F32),
            jax.ShapeDtypeStruct((t, ATT_WIDTH), BF16),
            jax.ShapeDtypeStruct((t * N_HEADS, V_DIM), F32),
            jax.ShapeDtypeStruct((t, ATT_WIDTH), BF16),
        ],
        compiler_params=_cparams(("parallel",)),
        name="proj_rows",
    )(x2d, w_bf16)


def _proj_prompt_kernel(qblk, vblk, x_ref, w_ref, wt_ref, u_ref, kf_ref, kb_ref, vf_ref, qt_ref, vt_ref, kn_ref):
    x = x_ref[...].astype(BF16)
    u_ref[...] = jnp.dot(x, w_ref[:, 0:C_Q], preferred_element_type=F32)
    k = jnp.dot(x, w_ref[:, C_Q:C_Q + ATT_WIDTH], preferred_element_type=F32)
    kb = k.astype(BF16)
    kb_ref[...] = kb
    kr = kb.astype(F32)
    member = (lax.broadcasted_iota(jnp.int32, (ATT_WIDTH, LANES), 0) >> 6) == lax.broadcasted_iota(
        jnp.int32, (ATT_WIDTH, LANES), 1)
    ksq = jnp.dot((kr * kr).astype(BF16), jnp.where(member, 1.0, 0.0).astype(BF16), preferred_element_type=F32)
    kn_ref[0] = jnp.broadcast_to(jnp.max(ksq, axis=0, keepdims=True), (SUBLANES, LANES))
    _store_heads(kf_ref, k)
    _store_heads(vf_ref, jnp.dot(x, w_ref[:, C_Q + ATT_WIDTH:], preferred_element_type=F32))
    nt = (((1,), (1,)), ((), ()))
    qt = lax.dot_general(wt_ref[0:ATT_WIDTH, :], x, nt, preferred_element_type=F32)
    qt = (qt * (QK_SCALE * LOG2E)).astype(BF16)
    vt = lax.dot_general(wt_ref[ATT_WIDTH:, :], x, nt, preferred_element_type=F32).astype(BF16)
    for j in range(x.shape[0] // qblk):
        qt_ref[j] = qt[:, j * qblk:(j + 1) * qblk]
    for j in range(x.shape[0] // vblk):
        vt_ref[j] = vt[:, j * vblk:(j + 1) * vblk]


def _project_prompt(x2d, w_rows, w_cols, tm, qblk, vblk):
    t = x2d.shape[0]
    row = lambda n: pl.BlockSpec((tm, n), lambda i: (i, 0))
    heads = pl.BlockSpec((tm * N_HEADS, V_DIM), lambda i: (i, 0))
    tblk = lambda n: pl.BlockSpec((tm // n, ATT_WIDTH, n), lambda i: (i, 0, 0))
    return pl.pallas_call(
        functools.partial(_proj_prompt_kernel, qblk, vblk),
        grid=(t // tm,),
        in_specs=[row(D_MODEL), _const_spec(w_rows.shape), _const_spec(w_cols.shape)],
        out_specs=[row(SSM_WIDTH), heads, row(ATT_WIDTH), heads, tblk(qblk), tblk(vblk),
                   pl.BlockSpec((1, SUBLANES, LANES), lambda i: (i, 0, 0))],
        out_shape=[
            jax.ShapeDtypeStruct((t, SSM_WIDTH), F32),
            jax.ShapeDtypeStruct((t * N_HEADS, V_DIM), F32),
            jax.ShapeDtypeStruct((t, ATT_WIDTH), BF16),
            jax.ShapeDtypeStruct((t * N_HEADS, V_DIM), F32),
            jax.ShapeDtypeStruct((t // qblk, ATT_WIDTH, qblk), BF16),
            jax.ShapeDtypeStruct((t // vblk, ATT_WIDTH, vblk), BF16),
            jax.ShapeDtypeStruct((t // tm, SUBLANES, LANES), F32),
        ],
        compiler_params=_cparams(("parallel",)),
        name="proj_prompt",
    )(x2d, w_rows, w_cols)


def _ssm_tables(lam_re, lam_im, log_dt, b_re, b_im, c_re, c_im):
    dt = jnp.exp(log_dt)[:, None]
    mag = jnp.exp(lam_re * dt)
    a_re, a_im = mag * jnp.cos(lam_im * dt), mag * jnp.sin(lam_im * dt)
    den = lam_re * lam_re + lam_im * lam_im
    zr, zi = a_re - 1.0, a_im
    fr = (zr * lam_re + zi * lam_im) / den
    fi = (zi * lam_re - zr * lam_im) / den
    bb_re = fr[..., None] * b_re - fi[..., None] * b_im
    bb_im = fr[..., None] * b_im + fi[..., None] * b_re
    eye = jnp.eye(SSM_GROUPS, dtype=F32)
    blk_b = lambda m: jnp.einsum('gpc,gh->gchp', m, eye).reshape(SSM_WIDTH, N_STATE).astype(BF16)
    blk_c = lambda m: jnp.einsum('gcp,gh->gphc', m, eye).reshape(N_STATE, SSM_WIDTH).astype(BF16)
    ar, ai = a_re.reshape(1, N_STATE), a_im.reshape(1, N_STATE)

    def cmul(xr, xi, yr, yi):
        return xr * yr - xi * yi, xr * yi + xi * yr

    rows = lax.broadcasted_iota(jnp.int32, (SUBLANES, 1), 0)
    tabs = []
    pr, pi = ar, ai
    for k in range(3):
        keep = rows >= (1 << k)
        tabs += [jnp.where(keep, pr, 0.0), jnp.where(keep, pi, 0.0)]
        pr, pi = cmul(pr, pi, pr, pi)
    cr, ci = [ar], [ai]
    for _ in range(SUBLANES - 1):
        nr, ni = cmul(cr[-1], ci[-1], ar, ai)
        cr.append(nr)
        ci.append(ni)
    tabs += [jnp.concatenate(cr, axis=0), jnp.concatenate(ci, axis=0)]
    return blk_b(bb_re), blk_b(bb_im), blk_c(c_re), blk_c(-c_im), jnp.stack(tabs)


SCAN_LANES = 512


def _scan_rows(sre, sim, mult_ref, r0, carry_re, carry_im):
    out_re, out_im = [], []
    for c0 in range(0, N_STATE, SCAN_LANES):
        ls = slice(c0, c0 + SCAN_LANES)
        xr = sre[pl.ds(r0, SUBLANES), ls]
        xi = sim[pl.ds(r0, SUBLANES), ls]
        for k in range(3):
            d = 1 << k
            yr = pltpu.roll(xr, d, 0)
            yi = pltpu.roll(xi, d, 0)
            mr = mult_ref[2 * k, :, ls]
            mi = mult_ref[2 * k + 1, :, ls]
            xr, xi = xr + (mr * yr - mi * yi), xi + (mr * yi + mi * yr)
        pr = mult_ref[6, :, ls]
        pi = mult_ref[7, :, ls]
        cr = carry_re[:, ls]
        ci = carry_im[:, ls]
        xr, xi = xr + (pr * cr - pi * ci), xi + (pr * ci + pi * cr)
        sre[pl.ds(r0, SUBLANES), ls] = xr
        sim[pl.ds(r0, SUBLANES), ls] = xi
        out_re.append(xr[SUBLANES - 1:SUBLANES, :])
        out_im.append(xi[SUBLANES - 1:SUBLANES, :])
    return jnp.concatenate(out_re, axis=1), jnp.concatenate(out_im, axis=1)


def _gelu_tanh(x):
    return 0.5 * x * (1.0 + jnp.tanh(math.sqrt(2.0 / math.pi) * (x + 0.044715 * (x * x * x))))


SSM_HALVES = 2


def _ssm_halves():
    cw, sw = SSM_WIDTH // SSM_HALVES, N_STATE // SSM_HALVES
    return [(slice(i * cw, (i + 1) * cw), slice(i * sw, (i + 1) * sw)) for i in range(SSM_HALVES)]


def _ssm_head(u_ref, bre_ref, bim_ref, sre, sim):
    ub = u_ref[...].astype(BF16)
    for cs, ss in _ssm_halves():
        sre[:, ss] = jnp.dot(ub[:, cs], bre_ref[cs, ss], preferred_element_type=F32)
        sim[:, ss] = jnp.dot(ub[:, cs], bim_ref[cs, ss], preferred_element_type=F32)


def _ssm_tail(u_ref, cre_ref, ncim_ref, d_ref, wglu_ref, bglu_ref, y_ref, sre, sim):
    y = jnp.concatenate(
        [jnp.dot(sre[:, ss].astype(BF16), cre_ref[ss, cs], preferred_element_type=F32)
         + jnp.dot(sim[:, ss].astype(BF16), ncim_ref[ss, cs], preferred_element_type=F32)
         for cs, ss in _ssm_halves()], axis=1) + d_ref[...] * u_ref[...]
    y = _gelu_tanh(y)
    z = jnp.dot(y.astype(BF16), wglu_ref[...], preferred_element_type=F32) + bglu_ref[...]
    y_ref[...] = (y * jax.nn.sigmoid(z)).astype(BF16)


def _ssm_seq_kernel(u_ref, bre_ref, bim_ref, cre_ref, ncim_ref, d_ref, wglu_ref, bglu_ref, mult_ref,
                    y_ref, ore_ref, oim_ref, sre, sim, car_re, car_im):
    @pl.when(pl.program_id(1) == 0)
    def _():
        car_re[...] = jnp.zeros_like(car_re)
        car_im[...] = jnp.zeros_like(car_im)

    _ssm_head(u_ref, bre_ref, bim_ref, sre, sim)
    n_blk = u_ref.shape[0] // SUBLANES

    def body(n, carry):
        r0 = pl.multiple_of(n * SUBLANES, SUBLANES)
        return _scan_rows(sre, sim, mult_ref, r0, carry[0], carry[1])

    cr, ci = lax.fori_loop(0, n_blk, body, (car_re[...], car_im[...]))
    car_re[...] = cr
    car_im[...] = ci
    ore_ref[0] = cr
    oim_ref[0] = ci
    _ssm_tail(u_ref, cre_ref, ncim_ref, d_ref, wglu_ref, bglu_ref, y_ref, sre, sim)


def _ssm_dec_kernel(u_ref, s0re_ref, s0im_ref, bre_ref, bim_ref, cre_ref, ncim_ref, d_ref, wglu_ref, bglu_ref,
                    mult_ref, y_ref, ore_ref, oim_ref, sre, sim):
    _ssm_head(u_ref, bre_ref, bim_ref, sre, sim)
    n_blk = u_ref.shape[0] // SUBLANES

    def body(n, _):
        r0 = pl.multiple_of(n * SUBLANES, SUBLANES)
        cr, ci = _scan_rows(sre, sim, mult_ref, r0, s0re_ref[pl.ds(n, 1), :], s0im_ref[pl.ds(n, 1), :])
        ore_ref[pl.ds(n, 1), :] = cr
        oim_ref[pl.ds(n, 1), :] = ci
        return 0

    lax.fori_loop(0, n_blk, body, 0)
    _ssm_tail(u_ref, cre_ref, ncim_ref, d_ref, wglu_ref, bglu_ref, y_ref, sre, sim)


def _ssm_weight_specs():
    return [_const_spec((SSM_WIDTH, N_STATE)), _const_spec((SSM_WIDTH, N_STATE)),
            _const_spec((N_STATE, SSM_WIDTH)), _const_spec((N_STATE, SSM_WIDTH)),
            _const_spec((1, SSM_WIDTH)), _const_spec((SSM_WIDTH, SSM_WIDTH)), _const_spec((1, SSM_WIDTH)),
            _const_spec((8, SUBLANES, N_STATE))]


def _ssm_prompt(u3, wts, lc):
    bsz, seq, _ = u3.shape
    return pl.pallas_call(
        _ssm_seq_kernel,
        grid=(bsz, seq // lc),
        in_specs=[pl.BlockSpec((None, lc, SSM_WIDTH), lambda b, c: (b, c, 0))] + _ssm_weight_specs(),
        out_specs=[pl.BlockSpec((None, lc, SSM_WIDTH), lambda b, c: (b, c, 0)),
                   pl.BlockSpec((1, 1, N_STATE), lambda b, c: (b, 0, 0)),
                   pl.BlockSpec((1, 1, N_STATE), lambda b, c: (b, 0, 0))],
        out_shape=[jax.ShapeDtypeStruct((bsz, seq, SSM_WIDTH), BF16),
                   jax.ShapeDtypeStruct((bsz, 1, N_STATE), F32),
                   jax.ShapeDtypeStruct((bsz, 1, N_STATE), F32)],
        scratch_shapes=[pltpu.VMEM((lc, N_STATE), F32), pltpu.VMEM((lc, N_STATE), F32),
                        pltpu.VMEM((1, N_STATE), F32), pltpu.VMEM((1, N_STATE), F32)],
        compiler_params=_cparams(("parallel", "arbitrary")),
        name="ssm_prompt",
    )(u3, *wts)


def _ssm_sample(u2, s0_re, s0_im, wts, lc):
    t = u2.shape[0]
    nseq = lc // SUBLANES
    return pl.pallas_call(
        _ssm_dec_kernel,
        grid=(t // lc,),
        in_specs=[pl.BlockSpec((lc, SSM_WIDTH), lambda c: (c, 0)),
                  pl.BlockSpec((nseq, N_STATE), lambda c: (c, 0)),
                  pl.BlockSpec((nseq, N_STATE), lambda c: (c, 0))] + _ssm_weight_specs(),
        out_specs=[pl.BlockSpec((lc, SSM_WIDTH), lambda c: (c, 0)),
                   pl.BlockSpec((nseq, N_STATE), lambda c: (c, 0)),
                   pl.BlockSpec((nseq, N_STATE), lambda c: (c, 0))],
        out_shape=[jax.ShapeDtypeStruct((t, SSM_WIDTH), BF16),
                   jax.ShapeDtypeStruct((t // SUBLANES, N_STATE), F32),
                   jax.ShapeDtypeStruct((t // SUBLANES, N_STATE), F32)],
        scratch_shapes=[pltpu.VMEM((lc, N_STATE), F32), pltpu.VMEM((lc, N_STATE), F32)],
        compiler_params=_cparams(("parallel",)),
        name="ssm_sample",
    )(u2, s0_re, s0_im, *wts)


def _head_rms(o, g, lam_init):
    ms = jnp.mean(o * o, axis=-1, keepdims=True)
    return o * lax.rsqrt(ms + LN_EPS) * g * (1.0 - lam_init)


N_BIAS_TERMS = 3
ATTN_BQ = 256
ATTN_BK = 256
ATTN_HEADS = 2
ATTN_UNROLL = 4
UNDERFLOW_BITS = 160.0
NORM_SLACK = 1.05
ONES_ROWS = 16


def _attn_kernel(lam_init, bq, bk, nh, slopes_ref, lamp_ref, g_ref, qt_ref, k_ref, vt_ref, kn_ref, o_ref,
                 acc_ref, m_ref, a_ref, u_ref, p_ref):
    hg = pl.program_id(1)
    qi = pl.program_id(2)
    two = 2 * bq
    n_diag = bq // bk
    heads = range(nh)
    hs = lambda hh: slice(hh * V_DIM, (hh + 1) * V_DIM)

    slope2, w, bias_cols = [], [], []
    r = lax.broadcasted_iota(jnp.int32, (V_DIM, bq), 0)
    ones_rows = lax.broadcasted_iota(jnp.int32, (V_DIM, two), 0) < N_BIAS_TERMS
    c = lax.broadcasted_iota(jnp.int32, (bk, V_DIM), 0).astype(F32)
    lane = lax.broadcasted_iota(jnp.int32, (bk, V_DIM), 1)
    for hh in heads:
        slope2.append(slopes_ref[hg * nh + hh] * LOG2E)
        qt = qt_ref[0, hs(hh), :]
        zero = jnp.zeros_like(qt)
        w_top = jnp.concatenate([jnp.where(r < HEAD_DIM, qt, zero), jnp.where(r >= HEAD_DIM, qt, zero)], axis=1)
        w.append(jnp.concatenate([w_top, jnp.where(ones_rows, 1.0, 0.0).astype(BF16)], axis=0))
        rem = slope2[hh] * c
        cols = jnp.zeros((bk, V_DIM), F32)
        for t in range(N_BIAS_TERMS):
            term = rem.astype(BF16).astype(F32)
            cols = jnp.where(lane == t, term, cols)
            rem = rem - term
        bias_cols.append(cols.astype(BF16))

    ksq_all = jnp.max(jnp.max(kn_ref[...], axis=0), axis=0, keepdims=True)
    klane = lax.broadcasted_iota(jnp.int32, (1, LANES), 1) >> 1
    first_f = None
    for hh in heads:
        kn = jnp.sqrt(jnp.max(jnp.where(klane == hg * nh + hh, ksq_all, 0.0), axis=1, keepdims=True))
        qf = qt_ref[0, hs(hh), :].astype(F32)
        q2 = qf * qf
        qsq = jnp.maximum(jnp.sum(q2[:HEAD_DIM], axis=0, keepdims=True), jnp.sum(q2[HEAD_DIM:], axis=0, keepdims=True))
        qn = jnp.sqrt(jnp.max(qsq, axis=1, keepdims=True))
        reach = (NORM_SLACK * 2.0 * qn * kn + UNDERFLOW_BITS) / slope2[hh] + bk
        f = jnp.floor((jnp.full((1, 1), qi * bq, jnp.int32).astype(F32) - reach) * (1.0 / bk)) + 1.0
        first_f = f if first_f is None else jnp.minimum(first_f, f)
    n_full = qi * n_diag
    first = jnp.clip(first_f, 0.0, 1e9).astype(jnp.int32)[0, 0]
    first = jnp.minimum(first & -2, n_full & -2)

    ones_rows16 = jnp.ones((ONES_ROWS, bk), BF16)

    m_ref[...] = jnp.full(m_ref.shape, NEG_INF, F32)
    a_ref[...] = jnp.ones(a_ref.shape, F32)
    acc_ref[...] = jnp.zeros(acc_ref.shape, F32)
    for hh in heads:
        p_ref[2 * hh + 1] = jnp.zeros((bk, two), BF16)

    def scores(hh, kj, slot):
        ks = pl.multiple_of(kj * bk, bk)
        kaug = jnp.concatenate([k_ref[pl.ds(ks, bk), hs(hh)], bias_cols[hh]], axis=1)
        u_ref[2 * hh + slot] = jnp.dot(kaug, w[hh], preferred_element_type=F32)

    def softmax(hh, kj, slot, diag):
        u = u_ref[2 * hh + slot]
        if diag is not None:
            keep = (lax.broadcasted_iota(jnp.int32, (bk, two), 1) & (bq - 1)) >= (
                lax.broadcasted_iota(jnp.int32, (bk, two), 0) + diag * bk)
            u = jnp.where(keep, u, NEG_INF)
        off = jnp.full((1, 1), kj * bk - qi * bq, jnp.int32).astype(F32) * slope2[hh]
        m_old = m_ref[hh]
        m_new = jnp.maximum(m_old, jnp.max(u, axis=0, keepdims=True) + off)
        a_ref[hh] = jnp.exp2(m_old - m_new)
        m_ref[hh] = m_new
        p_ref[2 * hh + slot] = jnp.exp2(u + (off - m_new)).astype(BF16)

    def values(hh, kj, slot):
        vaug = jnp.concatenate([vt_ref[jnp.maximum(kj, 0), hs(hh), :], ones_rows16], axis=0)
        acc_ref[hh] = a_ref[hh] * acc_ref[hh] + jnp.dot(vaug, p_ref[2 * hh + slot], preferred_element_type=F32)

    def run(kj, slot):
        for hh in heads:
            values(hh, kj - 1, 1 - slot)
            softmax(hh, kj, slot, None)
            scores(hh, kj + 1, 1 - slot)

    for hh in heads:
        scores(hh, first, 0)

    def body(i, carry):
        for d in range(ATTN_UNROLL):
            run(first + i * ATTN_UNROLL + d, d & 1)
        return carry

    n_run = n_full - first
    n_main = n_run // ATTN_UNROLL
    lax.fori_loop(0, n_main, body, 0)
    base = first + n_main * ATTN_UNROLL
    arm = ATTN_UNROLL // 2
    while arm >= 1:
        if arm > 1 or n_diag % 2 != 0:
            @pl.when((n_run & arm) != 0)
            def _(base=base, arm=arm):
                for d in range(arm):
                    run(base + d, d & 1)
            base = base + (n_run & arm)
        arm //= 2
    for d in range(n_diag):
        slot = d & 1 if n_diag % 2 == 0 else (n_full + d) & 1
        for hh in heads:
            values(hh, n_full + d - 1, 1 - slot)
            softmax(hh, n_full + d, slot, d)
            if d + 1 < n_diag:
                scores(hh, n_full + d + 1, 1 - slot)
    lam = _diff_lambda(lamp_ref[...], lam_init)
    for hh in heads:
        values(hh, n_full + n_diag - 1, slot)
        acc = acc_ref[hh]
        inv = 1.0 / acc[V_DIM:V_DIM + 1, :]
        acc = acc[:V_DIM, :]
        ot = acc[:, :bq] * inv[:, :bq] - lam * (acc[:, bq:] * inv[:, bq:])
        ms = jnp.mean(ot * ot, axis=0, keepdims=True)
        ot = ot * lax.rsqrt(ms + LN_EPS) * (g_ref[...] * (1.0 - lam_init))
        o_ref[0, :, hs(hh)] = ot.T.astype(o_ref.dtype)


def _attn_prompt(qt, k, vt, kn, slopes, lamp, g_col, lam_init, bsz, seq, bq, bk, nh):
    nq, nk = seq // bq, seq // bk
    hw = nh * V_DIM
    kn_tiles = kn.shape[0] // bsz
    return pl.pallas_call(
        functools.partial(_attn_kernel, lam_init, bq, bk, nh),
        grid=(bsz, N_HEADS // nh, nq),
        in_specs=[pl.BlockSpec(memory_space=pltpu.SMEM),
                  _const_spec((4, HEAD_DIM)), _const_spec((V_DIM, 1)),
                  pl.BlockSpec((1, hw, bq), lambda b, h, i: (b * nq + i, h, 0)),
                  pl.BlockSpec((seq, hw), lambda b, h, i: (b, h)),
                  pl.BlockSpec((nk, hw, bk), lambda b, h, i: (b, h, 0)),
                  pl.BlockSpec((kn_tiles, SUBLANES, LANES), lambda b, h, i: (b, 0, 0))],
        out_specs=pl.BlockSpec((1, bq, hw), lambda b, h, i: (b, i, h)),
        out_shape=jax.ShapeDtypeStruct((bsz, seq, ATT_WIDTH), BF16),
        scratch_shapes=[pltpu.VMEM((nh, V_DIM + ONES_ROWS, 2 * bq), F32),
                        pltpu.VMEM((nh, 1, 2 * bq), F32), pltpu.VMEM((nh, 1, 2 * bq), F32),
                        pltpu.VMEM((2 * nh, bk, 2 * bq), F32), pltpu.VMEM((2 * nh, bk, 2 * bq), BF16)],
        compiler_params=_cparams(("parallel", "parallel", "arbitrary")),
        name="attn_prompt",
    )(slopes, lamp, g_col, qt, k, vt, kn)


def _page_matrix(ref):
    return jnp.concatenate([ref[0, pl.ds(h, PAGE_SIZE, stride=N_HEADS), :].astype(BF16) for h in range(N_HEADS)],
                           axis=1)


def _sattn_kernel(lam_init, n_pages, pt_ref, lamp_ref, g_ref, q_ref, kn_ref, vn_ref, *rest):
    del pt_ref
    k_refs = rest[:n_pages]
    v_refs = rest[n_pages:2 * n_pages]
    o_ref = rest[2 * n_pages]
    acc_scr = rest[2 * n_pages + 1]
    t_new = q_ref.shape[1]
    n_past = n_pages * PAGE_SIZE
    nl = 2 * N_HEADS * t_new
    nt = (((1,), (1,)), ((), ()))
    tn = (((0,), (0,)), ((), ()))

    c = lax.broadcasted_iota(jnp.int32, (1, nl), 1)
    c_q = c & (t_new - 1)
    c_h = (c >> 3) & (N_HEADS - 1)
    c_m = c >> 6
    slope = jnp.exp2(-(c_h + 1).astype(F32))
    qpos = (n_past + c_q).astype(F32)

    r = lax.broadcasted_iota(jnp.int32, (nl, t_new), 0)
    sel = (r & (t_new - 1)) == lax.broadcasted_iota(jnp.int32, (nl, t_new), 1)
    rep = jnp.dot(sel.astype(BF16), q_ref[0], preferred_element_type=F32)
    rr = lax.broadcasted_iota(jnp.int32, (nl, ATT_WIDTH), 0)
    jj = lax.broadcasted_iota(jnp.int32, (nl, ATT_WIDTH), 1)
    same_head = ((rr >> 3) & (N_HEADS - 1)) == (jj >> 7)
    keep = same_head & ((rr >> 6) == ((jj >> 6) & 1))
    qbd = jnp.where(keep, rep, 0.0).astype(BF16)

    def rows_scaled(v):
        return jnp.concatenate([jnp.broadcast_to(v, (nl, nl)).T] * (ATT_WIDTH // nl), axis=1)

    dist = c_q - lax.broadcasted_iota(jnp.int32, (t_new, 1), 0)
    sn = lax.dot_general(kn_ref[0].astype(BF16), qbd, nt, preferred_element_type=F32)
    sn = jnp.where(dist >= 0, sn - slope * dist.astype(F32), NEG_INF)
    m = jnp.max(sn, axis=0, keepdims=True)
    pn = jnp.exp(sn - m)
    l = jnp.sum(pn, axis=0, keepdims=True)
    acc_scr[...] = lax.dot_general(pn.astype(BF16), vn_ref[0].astype(BF16), tn, preferred_element_type=F32)

    row = lax.broadcasted_iota(jnp.int32, (PAGE_SIZE, 1), 0).astype(F32)
    for i in range(n_pages):
        s = lax.dot_general(_page_matrix(k_refs[i]), qbd, nt, preferred_element_type=F32)
        s = s - slope * (qpos - (row + float(i * PAGE_SIZE)))
        m_new = jnp.maximum(m, jnp.max(s, axis=0, keepdims=True))
        alpha = jnp.exp(m - m_new)
        p = jnp.exp(s - m_new)
        l = alpha * l + jnp.sum(p, axis=0, keepdims=True)
        m = m_new
        pv = lax.dot_general(p.astype(BF16), _page_matrix(v_refs[i]), tn, preferred_element_type=F32)
        acc_scr[...] = acc_scr[...] * rows_scaled(alpha) + pv

    lam = _diff_lambda(lamp_ref[...], lam_init)
    coef = jnp.where(c_m == 0, 1.0, -lam) / l
    acc = jnp.where(same_head, acc_scr[...] * rows_scaled(coef), 0.0)
    o = jnp.sum(acc.reshape(nl // t_new, t_new, ATT_WIDTH), axis=0)
    for hh in range(N_HEADS):
        ls = slice(hh * V_DIM, (hh + 1) * V_DIM)
        o_ref[0, :, ls] = _head_rms(o[:, ls], g_ref[...], lam_init).astype(o_ref.dtype)


def _attn_sample(q, kn, vn, cache_k, cache_v, page_table, lamp, g, lam_init):
    nseq, t_new, _ = q.shape
    n_pages = page_table.shape[1]
    tok = pl.BlockSpec((1, t_new, ATT_WIDTH), lambda s, pt: (s, 0, 0))
    page = lambda i: pl.BlockSpec((1, PAGE_SIZE * N_HEADS, V_DIM), lambda s, pt: (pt[s, i], 0, 0))
    grid_spec = pltpu.PrefetchScalarGridSpec(
        num_scalar_prefetch=1,
        grid=(nseq,),
        in_specs=[pl.BlockSpec((4, HEAD_DIM), lambda s, pt: (0, 0)), pl.BlockSpec((1, V_DIM), lambda s, pt: (0, 0)),
                  tok, tok, tok]
                 + [page(i) for i in range(n_pages)] + [page(i) for i in range(n_pages)],
        out_specs=tok,
        scratch_shapes=[pltpu.VMEM((2 * N_HEADS * t_new, ATT_WIDTH), F32)],
    )
    return pl.pallas_call(
        functools.partial(_sattn_kernel, lam_init, n_pages),
        grid_spec=grid_spec,
        out_shape=jax.ShapeDtypeStruct((nseq, t_new, ATT_WIDTH), BF16),
        compiler_params=_cparams(("arbitrary",)),
        name="attn_sample",
    )(page_table, lamp, g, q, kn, vn, *([cache_k] * n_pages), *([cache_v] * n_pages))


def _merge_kernel(x_ref, ssm_ref, att_ref, wg_ref, bg_ref, pssm_ref, patt_ref, wo_ref, g_ref, b_ref, o_ref):
    x = x_ref[...]
    gates = jax.nn.sigmoid(jnp.dot(x.astype(BF16), wg_ref[...], preferred_element_type=F32) + bg_ref[...])
    merged = (gates[:, :D_MODEL] * jnp.dot(ssm_ref[...], pssm_ref[...], preferred_element_type=F32)
              + gates[:, D_MODEL:] * jnp.dot(att_ref[...], patt_ref[...], preferred_element_type=F32))
    y = DN_ALPHA * x + jnp.dot(merged.astype(BF16), wo_ref[...], preferred_element_type=F32)
    o_ref[...] = _layer_norm(y, g_ref[...], b_ref[...])


def _merge(x2d, ssm_y, att, wg, bg, pssm, patt, wo, g, b, tm):
    t = x2d.shape[0]
    row = lambda n: pl.BlockSpec((tm, n), lambda i: (i, 0))
    return pl.pallas_call(
        _merge_kernel,
        grid=(t // tm,),
        in_specs=[row(D_MODEL), row(SSM_WIDTH), row(ATT_WIDTH),
                  _const_spec((D_MODEL, 2 * D_MODEL)), _const_spec((1, 2 * D_MODEL)),
                  _const_spec((SSM_WIDTH, D_MODEL)), _const_spec((ATT_WIDTH, D_MODEL)),
                  _const_spec((D_MODEL, D_MODEL)), _const_spec((1, D_MODEL)), _const_spec((1, D_MODEL))],
        out_specs=row(D_MODEL),
        out_shape=jax.ShapeDtypeStruct((t, D_MODEL), F32),
        compiler_params=_cparams(("parallel",)),
        name="merge",
    )(x2d, ssm_y, att, wg, bg, pssm, patt, wo, g, b)


def _route(x_bf16, wrg, brg, wre, bre):
    gl = jnp.dot(x_bf16, wrg, preferred_element_type=F32) + brg
    el = jnp.dot(x_bf16, wre, preferred_element_type=F32) + bre
    gi = lax.broadcasted_iota(jnp.int32, gl.shape, 1).astype(F32)
    gmax = jnp.max(gl, axis=-1, keepdims=True)
    g_sel = jnp.min(jnp.where(gl == gmax, gi, float(N_EXPERT_GROUPS)), axis=-1, keepdims=True)
    p_g = 1.0 / jnp.sum(jnp.exp(gl - gmax), axis=-1, keepdims=True)
    ei_int = lax.broadcasted_iota(jnp.int32, el.shape, 1)
    ei = ei_int.astype(F32)
    el = jnp.where((ei_int >> 3).astype(F32) == g_sel, el, -jnp.inf)
    v1 = jnp.max(el, axis=-1, keepdims=True)
    i1 = jnp.min(jnp.where(el == v1, ei, float(N_EXPERTS)), axis=-1, keepdims=True)
    el2 = jnp.where(ei == i1, -jnp.inf, el)
    v2 = jnp.max(el2, axis=-1, keepdims=True)
    i2 = jnp.min(jnp.where(el2 == v2, ei, float(N_EXPERTS)), axis=-1, keepdims=True)
    e2 = jnp.exp(v2 - v1)
    w1 = p_g / (1.0 + e2)
    w2 = p_g * e2 / (1.0 + e2)
    lane = lax.broadcasted_iota(jnp.int32, (gl.shape[0], LANES), 1).astype(F32)
    first = EXPERTS_PER_GROUP * g_sel
    comb = jnp.where(lane == i1 - first, w1, 0.0) + jnp.where(lane == i2 - first, w2, 0.0)
    return jnp.where(lane == g_sel, 1.0, 0.0), comb


MOE_ROWS = 128
MOE_EXPERTS_PER_STEP = 4


def _moe_kernel(x_ref, wrg_ref, brg_ref, wre_ref, bre_ref, wg_ref, wu_ref, wd_ref, g_ref, b_ref, o_ref,
                xs_scr, comb_scr, acc_scr, unsort_scr, seg_ref):
    e = pl.program_id(1)
    tm = x_ref.shape[0]
    srows = xs_scr.shape[0]
    lane = lax.broadcasted_iota(jnp.int32, (1, LANES), 1)

    @pl.when(e == 0)
    def _():
        xb = x_ref[...].astype(BF16)
        group_1h, comb = _route(xb, wrg_ref[...], brg_ref[...], wre_ref[...], bre_ref[...])
        tri = lax.broadcasted_iota(jnp.int32, (tm, tm), 0) >= lax.broadcasted_iota(jnp.int32, (tm, tm), 1)
        count = jnp.dot(jnp.where(tri, 1.0, 0.0).astype(BF16), group_1h.astype(BF16), preferred_element_type=F32)
        total = count[tm - 1:tm, :]
        padded = jnp.floor((total + (MOE_ROWS - 1)) * (1.0 / MOE_ROWS)) * MOE_ROWS
        start = jnp.zeros((1, LANES), F32)
        for gidx in range(N_EXPERT_GROUPS):
            size = jnp.sum(jnp.where(lane == gidx, padded, 0.0), axis=-1, keepdims=True)
            seg_ref[0, gidx] = jnp.sum(jnp.where(lane == gidx, start, 0.0)).astype(jnp.int32) // MOE_ROWS
            seg_ref[1, gidx] = jnp.sum(size).astype(jnp.int32) // MOE_ROWS
            start = start + jnp.where(lane > gidx, size, 0.0)
        pos = jnp.sum(group_1h * (start + count - 1.0), axis=-1, keepdims=True)
        unsort = jnp.where(lax.broadcasted_iota(jnp.int32, (tm, srows), 1).astype(F32) == pos, 1.0, 0.0)
        unsort_scr[...] = unsort.astype(BF16)
        pos_row = jnp.broadcast_to(pos, (tm, LANES)).T[0:1, :]
        sort = jnp.where(lax.broadcasted_iota(jnp.int32, (srows, tm), 0).astype(F32) == pos_row, 1.0, 0.0)
        sort = sort.astype(BF16)
        xs_scr[...] = jnp.dot(sort, xb, preferred_element_type=F32).astype(BF16)
        comb_hi = comb.astype(BF16)
        comb_lo = (comb - comb_hi.astype(F32)).astype(BF16)
        comb_s = jnp.dot(sort, jnp.concatenate([comb_hi, comb_lo], axis=1), preferred_element_type=F32)
        comb_scr[...] = comb_s[:, :LANES] + comb_s[:, LANES:]
        acc_scr[...] = jnp.zeros_like(acc_scr)

    first_expert = e * MOE_EXPERTS_PER_STEP
    grp = first_expert >> 3
    in_grp = first_expert & (EXPERTS_PER_GROUP - 1)
    first_blk = seg_ref[0, grp]

    def block(i):
        r0 = pl.multiple_of((first_blk + i) * MOE_ROWS, MOE_ROWS)
        xs = xs_scr[pl.ds(r0, MOE_ROWS), :]
        comb = comb_scr[pl.ds(r0, MOE_ROWS), :]
        out = None
        for k in range(MOE_EXPERTS_PER_STEP):
            hg = jnp.dot(xs, wg_ref[k], preferred_element_type=F32)
            hu = jnp.dot(xs, wu_ref[k], preferred_element_type=F32)
            ce = jnp.sum(jnp.where(lane == in_grp + k, comb, 0.0), axis=-1, keepdims=True)
            hid = (hg * jax.nn.sigmoid(hg)) * hu * ce
            part = jnp.dot(hid.astype(BF16), wd_ref[k], preferred_element_type=F32)
            out = part if out is None else out + part
        acc_scr[pl.ds(r0, MOE_ROWS), :] += out

    n_blk = seg_ref[1, grp]

    def pair(i, carry):
        block(2 * i)
        block(2 * i + 1)
        return carry

    lax.fori_loop(0, n_blk >> 1, pair, 0)

    @pl.when((n_blk & 1) != 0)
    def _():
        block(n_blk - 1)

    @pl.when(e == N_EXPERTS // MOE_EXPERTS_PER_STEP - 1)
    def _():
        ffn = jnp.dot(unsort_scr[...], acc_scr[...].astype(BF16), preferred_element_type=F32)
        o_ref[...] = _layer_norm(DN_ALPHA * x_ref[...] + ffn, g_ref[...], b_ref[...])


def _moe(x1, wrg, brg, wre, bre, wg, wu, wd, g, b, tm):
    t = x1.shape[0]
    srows = tm + N_EXPERT_GROUPS * MOE_ROWS
    row = pl.BlockSpec((tm, D_MODEL), lambda i, e: (i, 0))
    return pl.pallas_call(
        _moe_kernel,
        grid=(t // tm, N_EXPERTS // MOE_EXPERTS_PER_STEP),
        in_specs=[row,
                  _const_spec((D_MODEL, N_EXPERT_GROUPS)), _const_spec((1, N_EXPERT_GROUPS)),
                  _const_spec((D_MODEL, N_EXPERTS)), _const_spec((1, N_EXPERTS)),
                  pl.BlockSpec((MOE_EXPERTS_PER_STEP, D_MODEL, EXPERT_FF), lambda i, e: (e, 0, 0)),
                  pl.BlockSpec((MOE_EXPERTS_PER_STEP, D_MODEL, EXPERT_FF), lambda i, e: (e, 0, 0)),
                  pl.BlockSpec((MOE_EXPERTS_PER_STEP, EXPERT_FF, D_MODEL), lambda i, e: (e, 0, 0)),
                  _const_spec((1, D_MODEL)), _const_spec((1, D_MODEL))],
        out_specs=row,
        out_shape=jax.ShapeDtypeStruct((t, D_MODEL), F32),
        scratch_shapes=[pltpu.VMEM((srows, D_MODEL), BF16), pltpu.VMEM((srows, LANES), F32),
                        pltpu.VMEM((srows, D_MODEL), F32), pltpu.VMEM((tm, srows), BF16),
                        pltpu.SMEM((2, N_EXPERT_GROUPS), jnp.int32)],
        compiler_params=_cparams(("parallel", "arbitrary")),
        name="moe",
    )(x1, wrg, brg, wre, bre, wg, wu, wd, g, b)


def kernel(x_prompt, x_sample, cache_k, cache_v, state_ssm_re, state_ssm_im, page_table, w_in, b_gate, ssm_lambda_re, ssm_lambda_im, ssm_log_dt, ssm_b_re, ssm_b_im, ssm_c_re, ssm_c_im, ssm_d, w_glu, b_glu, lambda_q1, lambda_k1, lambda_q2, lambda_k2, subln_g, p_ssm, p_att, w_o, ln1_g, ln1_b, w_router_group, b_router_group, w_router_expert, b_router_expert, w_exp_gate, w_exp_up, w_exp_down, ln2_g, ln2_b):
    bsz, seq, _ = x_prompt.shape
    nseq, t_new, _ = x_sample.shape
    assert w_in.shape[0] == DEPTH == 1 and t_new == SUBLANES
    l = 0
    lam_init = _lambda_init(l)
    row = lambda a: a[l].reshape(1, -1)

    w_qkv = w_in[l][:, :C_G].astype(BF16)
    w_gate = w_in[l][:, C_G:].astype(BF16)
    w_ukv = jnp.concatenate([w_qkv[:, :C_Q], w_qkv[:, C_K:]], axis=1)
    w_qv_t = jnp.concatenate([w_qkv[:, C_Q:C_K], w_qkv[:, C_V:]], axis=1).T
    ssm_tabs = _ssm_tables(ssm_lambda_re[l], ssm_lambda_im[l], ssm_log_dt[l], ssm_b_re[l], ssm_b_im[l],
                           ssm_c_re[l], ssm_c_im[l])
    bre, bim, cre, ncim, mult = ssm_tabs
    ssm_wts = (bre, bim, cre, ncim, row(ssm_d), w_glu[l].astype(BF16), row(b_glu), mult)
    lamp = jnp.stack([lambda_q1[l], lambda_k1[l], lambda_q2[l], lambda_k2[l]])
    g_sub = row(subln_g)
    slopes = jnp.exp2(-8.0 * (jnp.arange(N_HEADS, dtype=F32) + 1.0) / N_HEADS)
    merge_wts = (w_gate, row(b_gate), p_ssm[l].astype(BF16), p_att[l].astype(BF16), w_o[l].astype(BF16),
                 row(ln1_g), row(ln1_b))
    moe_wts = (w_router_group[l].astype(BF16), row(b_router_group), w_router_expert[l].astype(BF16),
               row(b_router_expert), w_exp_gate[l].astype(BF16), w_exp_up[l].astype(BF16),
               w_exp_down[l].astype(BF16), row(ln2_g), row(ln2_b))

    tp = bsz * seq
    xp = x_prompt.reshape(tp, D_MODEL)
    bq, bk = min(ATTN_BQ, seq), min(ATTN_BK, seq)
    u, kf, kb, vf, qt, vt, kn = _project_prompt(xp, w_ukv, w_qv_t, min(512, seq), bq, bk)
    ssm_y, srp, sip = _ssm_prompt(u.reshape(bsz, seq, SSM_WIDTH), ssm_wts, min(512, seq))
    att = _attn_prompt(qt, kb, vt, kn, slopes, lamp, subln_g[l].reshape(V_DIM, 1), lam_init, bsz, seq, bq, bk,
                       ATTN_HEADS)
    x1 = _merge(xp, ssm_y.reshape(tp, SSM_WIDTH), att.reshape(tp, ATT_WIDTH), *merge_wts, min(256, tp))
    y_prompt = _moe(x1, *moe_wts, min(1024, tp)).reshape(bsz, seq, D_MODEL)
    k_prompt = kf.reshape(1, bsz, seq, N_HEADS, V_DIM)
    v_prompt = vf.reshape(1, bsz, seq, N_HEADS, V_DIM)
    srp = srp.reshape(1, bsz, SSM_GROUPS, SSM_STATE)
    sip = sip.reshape(1, bsz, SSM_GROUPS, SSM_STATE)

    ts = nseq * t_new
    xs = x_sample.reshape(ts, D_MODEL)
    u, q, kf, kb, vf, vb = _project_rows(xs, w_qkv, min(512, ts))
    ssm_y, srs, sis = _ssm_sample(u, state_ssm_re[l].reshape(nseq, N_STATE), state_ssm_im[l].reshape(nseq, N_STATE),
                                  ssm_wts, min(512, ts))
    att = _attn_sample(q.reshape(nseq, t_new, ATT_WIDTH), kb.reshape(nseq, t_new, ATT_WIDTH),
                       vb.reshape(nseq, t_new, ATT_WIDTH),
                       cache_k[l].reshape(-1, PAGE_SIZE * N_HEADS, V_DIM),
                       cache_v[l].reshape(-1, PAGE_SIZE * N_HEADS, V_DIM), page_table, lamp, g_sub, lam_init)
    x1 = _merge(xs, ssm_y, att.reshape(ts, ATT_WIDTH), *merge_wts, min(256, ts))
    y_sample = _moe(x1, *moe_wts, min(1024, ts)).reshape(nseq, t_new, D_MODEL)
    k_sample = kf.reshape(1, nseq, t_new, N_HEADS, V_DIM)
    v_sample = vf.reshape(1, nseq, t_new, N_HEADS, V_DIM)
    srs = srs.reshape(1, nseq, SSM_GROUPS, SSM_STATE)
    sis = sis.reshape(1, nseq, SSM_GROUPS, SSM_STATE)

    return (y_prompt, y_sample, k_prompt, v_prompt, srp, sip, k_sample, v_sample, srs, sis)
```

```python
import functools
import math
from typing import NamedTuple

import jax
import jax.numpy as jnp
from jax import lax
from jax.experimental import pallas as pl
from jax.experimental.pallas import tpu as pltpu

F32 = jnp.float32
BF16 = jnp.bfloat16

D_MODEL = 1024
N_HEADS = 8
HEAD_DIM = 64
V_DIM = 2 * HEAD_DIM
ATT_WIDTH = N_HEADS * V_DIM
SSM_WIDTH = 512
SSM_GROUP = 16
SSM_GROUPS = SSM_WIDTH // SSM_GROUP
SSM_STATE = 64
N_STATE = SSM_GROUPS * SSM_STATE
N_EXPERT_GROUPS = 4
EXPERTS_PER_GROUP = 8
N_EXPERTS = N_EXPERT_GROUPS * EXPERTS_PER_GROUP
EXPERT_FF = 256
PAGE_SIZE = 128
DEPTH = 1
DN_ALPHA = (2.0 * DEPTH) ** 0.25
LN_EPS = 1e-5
NEG_INF = -1e30
QK_SCALE = HEAD_DIM ** -0.5
LOG2E = math.log2(math.e)
C_Q = SSM_WIDTH
C_K = C_Q + ATT_WIDTH
C_V = C_K + ATT_WIDTH
C_G = C_V + ATT_WIDTH

VMEM_LIMIT_BYTES = 52 * 1024 * 1024
SUBLANES = 8
LANES = 128


def _cparams(sem):
    return pltpu.CompilerParams(dimension_semantics=sem, vmem_limit_bytes=VMEM_LIMIT_BYTES)


def _const_spec(shape):
    nd = len(shape)
    return pl.BlockSpec(shape, lambda *_: (0,) * nd)


def _layer_norm(x, g, b):
    mu = jnp.mean(x, axis=-1, keepdims=True)
    xc = x - mu
    var = jnp.mean(xc * xc, axis=-1, keepdims=True)
    return xc * lax.rsqrt(var + LN_EPS) * g + b


def _lambda_init(layer_idx):
    return 0.8 - 0.6 * math.exp(-0.3 * layer_idx)


def _diff_lambda(lamp, lam_init):
    a = jnp.sum(lamp[0:1, :] * lamp[1:2, :], axis=-1, keepdims=True)
    b = jnp.sum(lamp[2:3, :] * lamp[3:4, :], axis=-1, keepdims=True)
    return jnp.exp(a) - jnp.exp(b) + lam_init


def _store_heads(ref, val):
    n = val.shape[0]
    for h in range(N_HEADS):
        ref[pl.ds(h, n, stride=N_HEADS), :] = val[:, h * V_DIM:(h + 1) * V_DIM]


def _proj_rows_kernel(x_ref, w_ref, u_ref, q_ref, kf_ref, kb_ref, vf_ref, vb_ref):
    x = x_ref[...].astype(BF16)
    u_ref[...] = jnp.dot(x, w_ref[:, 0:C_Q], preferred_element_type=F32)
    q = jnp.dot(x, w_ref[:, C_Q:C_K], preferred_element_type=F32)
    q_ref[...] = (q * QK_SCALE).astype(BF16)
    k = jnp.dot(x, w_ref[:, C_K:C_V], preferred_element_type=F32)
    kb_ref[...] = k.astype(BF16)
    _store_heads(kf_ref, k)
    v = jnp.dot(x, w_ref[:, C_V:C_G], preferred_element_type=F32)
    vb_ref[...] = v.astype(BF16)
    _store_heads(vf_ref, v)


def _project_rows(x2d, w_bf16, tm):
    t = x2d.shape[0]
    row = lambda n: pl.BlockSpec((tm, n), lambda i: (i, 0))
    heads = pl.BlockSpec((tm * N_HEADS, V_DIM), lambda i: (i, 0))
    return pl.pallas_call(
        _proj_rows_kernel,
        grid=(t // tm,),
        in_specs=[row(D_MODEL), _const_spec((D_MODEL, C_G))],
        out_specs=[row(SSM_WIDTH), row(ATT_WIDTH), heads, row(ATT_WIDTH), heads, row(ATT_WIDTH)],
        out_shape=[
            jax.ShapeDtypeStruct((t, SSM_WIDTH), F32),
            jax.ShapeDtypeStruct((t, ATT_WIDTH), BF16),
            jax.ShapeDtypeStruct((t * N_HEADS, V_DIM), F32),
            jax.ShapeDtypeStruct((t, ATT_WIDTH), BF16),
            jax.ShapeDtypeStruct((t * N_HEADS, V_DIM), F32),
            jax.ShapeDtypeStruct((t, ATT_WIDTH), BF16),
        ],
        compiler_params=_cparams(("parallel",)),
        name="proj_rows",
    )(x2d, w_bf16)


def _proj_prompt_kernel(qblk, vblk, x_ref, w_ref, wt_ref, u_ref, kf_ref, kb_ref, vf_ref, qt_ref, vt_ref, kn_ref):
    x = x_ref[...].astype(BF16)
    u_ref[...] = jnp.dot(x, w_ref[:, 0:C_Q], preferred_element_type=F32)
    k = jnp.dot(x, w_ref[:, C_Q:C_Q + ATT_WIDTH], preferred_element_type=F32)
    kb = k.astype(BF16)
    kb_ref[...] = kb
    kr = kb.astype(F32)
    member = (lax.broadcasted_iota(jnp.int32, (ATT_WIDTH, LANES), 0) >> 6) == lax.broadcasted_iota(
        jnp.int32, (ATT_WIDTH, LANES), 1)
    ksq = jnp.dot((kr * kr).astype(BF16), jnp.where(member, 1.0, 0.0).astype(BF16), preferred_element_type=F32)
    kn_ref[0] = jnp.broadcast_to(jnp.max(ksq, axis=0, keepdims=True), (SUBLANES, LANES))
    _store_heads(kf_ref, k)
    _store_heads(vf_ref, jnp.dot(x, w_ref[:, C_Q + ATT_WIDTH:], preferred_element_type=F32))
    nt = (((1,), (1,)), ((), ()))
    qt = lax.dot_general(wt_ref[0:ATT_WIDTH, :], x, nt, preferred_element_type=F32)
    qt = (qt * (QK_SCALE * LOG2E)).astype(BF16)
    vt = lax.dot_general(wt_ref[ATT_WIDTH:, :], x, nt, preferred_element_type=F32).astype(BF16)
    for j in range(x.shape[0] // qblk):
        qt_ref[j] = qt[:, j * qblk:(j + 1) * qblk]
    for j in range(x.shape[0] // vblk):
        vt_ref[j] = vt[:, j * vblk:(j + 1) * vblk]


def _project_prompt(x2d, w_rows, w_cols, tm, qblk, vblk):
    t = x2d.shape[0]
    row = lambda n: pl.BlockSpec((tm, n), lambda i: (i, 0))
    heads = pl.BlockSpec((tm * N_HEADS, V_DIM), lambda i: (i, 0))
    tblk = lambda n: pl.BlockSpec((tm // n, ATT_WIDTH, n), lambda i: (i, 0, 0))
    return pl.pallas_call(
        functools.partial(_proj_prompt_kernel, qblk, vblk),
        grid=(t // tm,),
        in_specs=[row(D_MODEL), _const_spec(w_rows.shape), _const_spec(w_cols.shape)],
        out_specs=[row(SSM_WIDTH), heads, row(ATT_WIDTH), heads, tblk(qblk), tblk(vblk),
                   pl.BlockSpec((1, SUBLANES, LANES), lambda i: (i, 0, 0))],
        out_shape=[
            jax.ShapeDtypeStruct((t, SSM_WIDTH), F32),
            jax.ShapeDtypeStruct((t * N_HEADS, V_DIM), F32),
            jax.ShapeDtypeStruct((t, ATT_WIDTH), BF16),
            jax.ShapeDtypeStruct((t * N_HEADS, V_DIM), F32),
            jax.ShapeDtypeStruct((t // qblk, ATT_WIDTH, qblk), BF16),
            jax.ShapeDtypeStruct((t // vblk, ATT_WIDTH, vblk), BF16),
            jax.ShapeDtypeStruct((t // tm, SUBLANES, LANES), F32),
        ],
        compiler_params=_cparams(("parallel",)),
        name="proj_prompt",
    )(x2d, w_rows, w_cols)


def _ssm_tables(lam_re, lam_im, log_dt, b_re, b_im, c_re, c_im):
    dt = jnp.exp(log_dt)[:, None]
    mag = jnp.exp(lam_re * dt)
    a_re, a_im = mag * jnp.cos(lam_im * dt), mag * jnp.sin(lam_im * dt)
    den = lam_re * lam_re + lam_im * lam_im
    zr, zi = a_re - 1.0, a_im
    fr = (zr * lam_re + zi * lam_im) / den
    fi = (zi * lam_re - zr * lam_im) / den
    bb_re = fr[..., None] * b_re - fi[..., None] * b_im
    bb_im = fr[..., None] * b_im + fi[..., None] * b_re
    eye = jnp.eye(SSM_GROUPS, dtype=F32)
    blk_b = lambda m: jnp.einsum('gpc,gh->gchp', m, eye).reshape(SSM_WIDTH, N_STATE).astype(BF16)
    blk_c = lambda m: jnp.einsum('gcp,gh->gphc', m, eye).reshape(N_STATE, SSM_WIDTH).astype(BF16)
    ar, ai = a_re.reshape(1, N_STATE), a_im.reshape(1, N_STATE)

    def cmul(xr, xi, yr, yi):
        return xr * yr - xi * yi, xr * yi + xi * yr

    rows = lax.broadcasted_iota(jnp.int32, (SUBLANES, 1), 0)
    tabs = []
    pr, pi = ar, ai
    for k in range(3):
        keep = rows >= (1 << k)
        tabs += [jnp.where(keep, pr, 0.0), jnp.where(keep, pi, 0.0)]
        pr, pi = cmul(pr, pi, pr, pi)
    cr, ci = [ar], [ai]
    for _ in range(SUBLANES - 1):
        nr, ni = cmul(cr[-1], ci[-1], ar, ai)
        cr.append(nr)
        ci.append(ni)
    tabs += [jnp.concatenate(cr, axis=0), jnp.concatenate(ci, axis=0)]
    return blk_b(bb_re), blk_b(bb_im), blk_c(c_re), blk_c(-c_im), jnp.stack(tabs)


SCAN_LANES = 512


def _scan_rows(sre, sim, mult_ref, r0, carry_re, carry_im):
    out_re, out_im = [], []
    for c0 in range(0, N_STATE, SCAN_LANES):
        ls = slice(c0, c0 + SCAN_LANES)
        xr = sre[pl.ds(r0, SUBLANES), ls]
        xi = sim[pl.ds(r0, SUBLANES), ls]
        for k in range(3):
            d = 1 << k
            yr = pltpu.roll(xr, d, 0)
            yi = pltpu.roll(xi, d, 0)
            mr = mult_ref[2 * k, :, ls]
            mi = mult_ref[2 * k + 1, :, ls]
            xr, xi = xr + (mr * yr - mi * yi), xi + (mr * yi + mi * yr)
        pr = mult_ref[6, :, ls]
        pi = mult_ref[7, :, ls]
        cr = carry_re[:, ls]
        ci = carry_im[:, ls]
        xr, xi = xr + (pr * cr - pi * ci), xi + (pr * ci + pi * cr)
        sre[pl.ds(r0, SUBLANES), ls] = xr
        sim[pl.ds(r0, SUBLANES), ls] = xi
        out_re.append(xr[SUBLANES - 1:SUBLANES, :])
        out_im.append(xi[SUBLANES - 1:SUBLANES, :])
    return jnp.concatenate(out_re, axis=1), jnp.concatenate(out_im, axis=1)


def _gelu_tanh(x):
    return 0.5 * x * (1.0 + jnp.tanh(math.sqrt(2.0 / math.pi) * (x + 0.044715 * (x * x * x))))


SSM_HALVES = 2


def _ssm_halves():
    cw, sw = SSM_WIDTH // SSM_HALVES, N_STATE // SSM_HALVES
    return [(slice(i * cw, (i + 1) * cw), slice(i * sw, (i + 1) * sw)) for i in range(SSM_HALVES)]


def _ssm_head(u_ref, bre_ref, bim_ref, sre, sim):
    ub = u_ref[...].astype(BF16)
    for cs, ss in _ssm_halves():
        sre[:, ss] = jnp.dot(ub[:, cs], bre_ref[cs, ss], preferred_element_type=F32)
        sim[:, ss] = jnp.dot(ub[:, cs], bim_ref[cs, ss], preferred_element_type=F32)


def _ssm_tail(u_ref, cre_ref, ncim_ref, d_ref, wglu_ref, bglu_ref, y_ref, sre, sim):
    y = jnp.concatenate(
        [jnp.dot(sre[:, ss].astype(BF16), cre_ref[ss, cs], preferred_element_type=F32)
         + jnp.dot(sim[:, ss].astype(BF16), ncim_ref[ss, cs], preferred_element_type=F32)
         for cs, ss in _ssm_halves()], axis=1) + d_ref[...] * u_ref[...]
    y = _gelu_tanh(y)
    z = jnp.dot(y.astype(BF16), wglu_ref[...], preferred_element_type=F32) + bglu_ref[...]
    y_ref[...] = (y * jax.nn.sigmoid(z)).astype(BF16)


def _ssm_seq_kernel(u_ref, bre_ref, bim_ref, cre_ref, ncim_ref, d_ref, wglu_ref, bglu_ref, mult_ref,
                    y_ref, ore_ref, oim_ref, sre, sim, car_re, car_im):
    @pl.when(pl.program_id(1) == 0)
    def _():
        car_re[...] = jnp.zeros_like(car_re)
        car_im[...] = jnp.zeros_like(car_im)

    _ssm_head(u_ref, bre_ref, bim_ref, sre, sim)
    n_blk = u_ref.shape[0] // SUBLANES

    def body(n, carry):
        r0 = pl.multiple_of(n * SUBLANES, SUBLANES)
        return _scan_rows(sre, sim, mult_ref, r0, carry[0], carry[1])

    cr, ci = lax.fori_loop(0, n_blk, body, (car_re[...], car_im[...]))
    car_re[...] = cr
    car_im[...] = ci
    ore_ref[0] = cr
    oim_ref[0] = ci
    _ssm_tail(u_ref, cre_ref, ncim_ref, d_ref, wglu_ref, bglu_ref, y_ref, sre, sim)


def _ssm_dec_kernel(u_ref, s0re_ref, s0im_ref, bre_ref, bim_ref, cre_ref, ncim_ref, d_ref, wglu_ref, bglu_ref,
                    mult_ref, y_ref, ore_ref, oim_ref, sre, sim):
    _ssm_head(u_ref, bre_ref, bim_ref, sre, sim)
    n_blk = u_ref.shape[0] // SUBLANES

    def body(n, _):
        r0 = pl.multiple_of(n * SUBLANES, SUBLANES)
        cr, ci = _scan_rows(sre, sim, mult_ref, r0, s0re_ref[pl.ds(n, 1), :], s0im_ref[pl.ds(n, 1), :])
        ore_ref[pl.ds(n, 1), :] = cr
        oim_ref[pl.ds(n, 1), :] = ci
        return 0

    lax.fori_loop(0, n_blk, body, 0)
    _ssm_tail(u_ref, cre_ref, ncim_ref, d_ref, wglu_ref, bglu_ref, y_ref, sre, sim)


def _ssm_weight_specs():
    return [_const_spec((SSM_WIDTH, N_STATE)), _const_spec((SSM_WIDTH, N_STATE)),
            _const_spec((N_STATE, SSM_WIDTH)), _const_spec((N_STATE, SSM_WIDTH)),
            _const_spec((1, SSM_WIDTH)), _const_spec((SSM_WIDTH, SSM_WIDTH)), _const_spec((1, SSM_WIDTH)),
            _const_spec((8, SUBLANES, N_STATE))]


def _ssm_prompt(u3, wts, lc):
    bsz, seq, _ = u3.shape
    return pl.pallas_call(
        _ssm_seq_kernel,
        grid=(bsz, seq // lc),
        in_specs=[pl.BlockSpec((None, lc, SSM_WIDTH), lambda b, c: (b, c, 0))] + _ssm_weight_specs(),
        out_specs=[pl.BlockSpec((None, lc, SSM_WIDTH), lambda b, c: (b, c, 0)),
                   pl.BlockSpec((1, 1, N_STATE), lambda b, c: (b, 0, 0)),
                   pl.BlockSpec((1, 1, N_STATE), lambda b, c: (b, 0, 0))],
        out_shape=[jax.ShapeDtypeStruct((bsz, seq, SSM_WIDTH), BF16),
                   jax.ShapeDtypeStruct((bsz, 1, N_STATE), F32),
                   jax.ShapeDtypeStruct((bsz, 1, N_STATE), F32)],
        scratch_shapes=[pltpu.VMEM((lc, N_STATE), F32), pltpu.VMEM((lc, N_STATE), F32),
                        pltpu.VMEM((1, N_STATE), F32), pltpu.VMEM((1, N_STATE), F32)],
        compiler_params=_cparams(("parallel", "arbitrary")),
        name="ssm_prompt",
    )(u3, *wts)


def _ssm_sample(u2, s0_re, s0_im, wts, lc):
    t = u2.shape[0]
    nseq = lc // SUBLANES
    return pl.pallas_call(
        _ssm_dec_kernel,
        grid=(t // lc,),
        in_specs=[pl.BlockSpec((lc, SSM_WIDTH), lambda c: (c, 0)),
                  pl.BlockSpec((nseq, N_STATE), lambda c: (c, 0)),
                  pl.BlockSpec((nseq, N_STATE), lambda c: (c, 0))] + _ssm_weight_specs(),
        out_specs=[pl.BlockSpec((lc, SSM_WIDTH), lambda c: (c, 0)),
                   pl.BlockSpec((nseq, N_STATE), lambda c: (c, 0)),
                   pl.BlockSpec((nseq, N_STATE), lambda c: (c, 0))],
        out_shape=[jax.ShapeDtypeStruct((t, SSM_WIDTH), BF16),
                   jax.ShapeDtypeStruct((t // SUBLANES, N_STATE), F32),
                   jax.ShapeDtypeStruct((t // SUBLANES, N_STATE), F32)],
        scratch_shapes=[pltpu.VMEM((lc, N_STATE), F32), pltpu.VMEM((lc, N_STATE), F32)],
        compiler_params=_cparams(("parallel",)),
        name="ssm_sample",
    )(u2, s0_re, s0_im, *wts)


def _head_rms(o, g, lam_init):
    ms = jnp.mean(o * o, axis=-1, keepdims=True)
    return o * lax.rsqrt(ms + LN_EPS) * g * (1.0 - lam_init)


N_BIAS_TERMS = 3
ATTN_BQ = 256
ATTN_BK = 256
ATTN_HEADS = 4
ATTN_UNROLL = 4
UNDERFLOW_BITS = 160.0
NORM_SLACK = 1.05
ONES_ROWS = 16


def _attn_kernel(lam_init, bq, bk, nh, slopes_ref, lamp_ref, g_ref, qt_ref, k_ref, vt_ref, kn_ref, o_ref,
                 acc_ref, m_ref, a_ref, u_ref, p_ref):
    hg = pl.program_id(1)
    qi = pl.program_id(2)
    two = 2 * bq
    n_diag = bq // bk
    heads = range(nh)
    hs = lambda hh: slice(hh * V_DIM, (hh + 1) * V_DIM)

    slope2, w, bias_cols = [], [], []
    r = lax.broadcasted_iota(jnp.int32, (V_DIM, bq), 0)
    ones_rows = lax.broadcasted_iota(jnp.int32, (V_DIM, two), 0) < N_BIAS_TERMS
    c = lax.broadcasted_iota(jnp.int32, (bk, V_DIM), 0).astype(F32)
    lane = lax.broadcasted_iota(jnp.int32, (bk, V_DIM), 1)
    for hh in heads:
        slope2.append(slopes_ref[hg * nh + hh] * LOG2E)
        qt = qt_ref[0, hs(hh), :]
        zero = jnp.zeros_like(qt)
        w_top = jnp.concatenate([jnp.where(r < HEAD_DIM, qt, zero), jnp.where(r >= HEAD_DIM, qt, zero)], axis=1)
        w.append(jnp.concatenate([w_top, jnp.where(ones_rows, 1.0, 0.0).astype(BF16)], axis=0))
        rem = slope2[hh] * c
        cols = jnp.zeros((bk, V_DIM), F32)
        for t in range(N_BIAS_TERMS):
            term = rem.astype(BF16).astype(F32)
            cols = jnp.where(lane == t, term, cols)
            rem = rem - term
        bias_cols.append(cols.astype(BF16))

    ksq_all = jnp.max(jnp.max(kn_ref[...], axis=0), axis=0, keepdims=True)
    klane = lax.broadcasted_iota(jnp.int32, (1, LANES), 1) >> 1
    first_f = None
    for hh in heads:
        kn = jnp.sqrt(jnp.max(jnp.where(klane == hg * nh + hh, ksq_all, 0.0), axis=1, keepdims=True))
        qf = qt_ref[0, hs(hh), :].astype(F32)
        q2 = qf * qf
        qsq = jnp.maximum(jnp.sum(q2[:HEAD_DIM], axis=0, keepdims=True), jnp.sum(q2[HEAD_DIM:], axis=0, keepdims=True))
        qn = jnp.sqrt(jnp.max(qsq, axis=1, keepdims=True))
        reach = (NORM_SLACK * 2.0 * qn * kn + UNDERFLOW_BITS) / slope2[hh] + bk
        f = jnp.floor((jnp.full((1, 1), qi * bq, jnp.int32).astype(F32) - reach) * (1.0 / bk)) + 1.0
        first_f = f if first_f is None else jnp.minimum(first_f, f)
    n_full = qi * n_diag
    first = jnp.clip(first_f, 0.0, 1e9).astype(jnp.int32)[0, 0]
    first = jnp.minimum(first & -2, n_full & -2)

    ones_rows16 = jnp.ones((ONES_ROWS, bk), BF16)

    m_ref[...] = jnp.full(m_ref.shape, NEG_INF, F32)
    a_ref[...] = jnp.ones(a_ref.shape, F32)
    acc_ref[...] = jnp.zeros(acc_ref.shape, F32)
    for hh in heads:
        p_ref[2 * hh + 1] = jnp.zeros((bk, two), BF16)

    def scores(hh, kj, slot):
        ks = pl.multiple_of(kj * bk, bk)
        kaug = jnp.concatenate([k_ref[pl.ds(ks, bk), hs(hh)], bias_cols[hh]], axis=1)
        u_ref[2 * hh + slot] = jnp.dot(kaug, w[hh], preferred_element_type=F32)

    def softmax(hh, kj, slot, diag):
        u = u_ref[2 * hh + slot]
        if diag is not None:
            keep = (lax.broadcasted_iota(jnp.int32, (bk, two), 1) & (bq - 1)) >= (
                lax.broadcasted_iota(jnp.int32, (bk, two), 0) + diag * bk)
            u = jnp.where(keep, u, NEG_INF)
        off = jnp.full((1, 1), kj * bk - qi * bq, jnp.int32).astype(F32) * slope2[hh]
        m_old = m_ref[hh]
        m_new = jnp.maximum(m_old, jnp.max(u, axis=0, keepdims=True) + off)
        a_ref[hh] = jnp.exp2(m_old - m_new)
        m_ref[hh] = m_new
        p_ref[2 * hh + slot] = jnp.exp2(u + (off - m_new)).astype(BF16)

    def values(hh, kj, slot):
        vaug = jnp.concatenate([vt_ref[jnp.maximum(kj, 0), hs(hh), :], ones_rows16], axis=0)
        acc_ref[hh] = a_ref[hh] * acc_ref[hh] + jnp.dot(vaug, p_ref[2 * hh + slot], preferred_element_type=F32)

    def run(kj, slot):
        for hh in heads:
            values(hh, kj - 1, 1 - slot)
            softmax(hh, kj, slot, None)
            scores(hh, kj + 1, 1 - slot)

    for hh in heads:
        scores(hh, first, 0)

    def body(i, carry):
        for d in range(ATTN_UNROLL):
            run(first + i * ATTN_UNROLL + d, d & 1)
        return carry

    n_run = n_full - first
    n_main = n_run // ATTN_UNROLL
    lax.fori_loop(0, n_main, body, 0)
    base = first + n_main * ATTN_UNROLL
    arm = ATTN_UNROLL // 2
    while arm >= 1:
        if arm > 1 or n_diag % 2 != 0:
            @pl.when((n_run & arm) != 0)
            def _(base=base, arm=arm):
                for d in range(arm):
                    run(base + d, d & 1)
            base = base + (n_run & arm)
        arm //= 2
    for d in range(n_diag):
        slot = d & 1 if n_diag % 2 == 0 else (n_full + d) & 1
        for hh in heads:
            values(hh, n_full + d - 1, 1 - slot)
            softmax(hh, n_full + d, slot, d)
            if d + 1 < n_diag:
                scores(hh, n_full + d + 1, 1 - slot)
    lam = _diff_lambda(lamp_ref[...], lam_init)
    for hh in heads:
        values(hh, n_full + n_diag - 1, slot)
        acc = acc_ref[hh]
        inv = 1.0 / acc[V_DIM:V_DIM + 1, :]
        acc = acc[:V_DIM, :]
        ot = acc[:, :bq] * inv[:, :bq] - lam * (acc[:, bq:] * inv[:, bq:])
        ms = jnp.mean(ot * ot, axis=0, keepdims=True)
        ot = ot * lax.rsqrt(ms + LN_EPS) * (g_ref[...] * (1.0 - lam_init))
        o_ref[0, :, hs(hh)] = ot.T.astype(o_ref.dtype)


def _attn_prompt(qt, k, vt, kn, slopes, lamp, g_col, lam_init, bsz, seq, bq, bk, nh):
    nq, nk = seq // bq, seq // bk
    hw = nh * V_DIM
    kn_tiles = kn.shape[0] // bsz
    return pl.pallas_call(
        functools.partial(_attn_kernel, lam_init, bq, bk, nh),
        grid=(bsz, N_HEADS // nh, nq),
        in_specs=[pl.BlockSpec(memory_space=pltpu.SMEM),
                  _const_spec((4, HEAD_DIM)), _const_spec((V_DIM, 1)),
                  pl.BlockSpec((1, hw, bq), lambda b, h, i: (b * nq + i, h, 0)),
                  pl.BlockSpec((seq, hw), lambda b, h, i: (b, h)),
                  pl.BlockSpec((nk, hw, bk), lambda b, h, i: (b, h, 0)),
                  pl.BlockSpec((kn_tiles, SUBLANES, LANES), lambda b, h, i: (b, 0, 0))],
        out_specs=pl.BlockSpec((1, bq, hw), lambda b, h, i: (b, i, h)),
        out_shape=jax.ShapeDtypeStruct((bsz, seq, ATT_WIDTH), BF16),
        scratch_shapes=[pltpu.VMEM((nh, V_DIM + ONES_ROWS, 2 * bq), F32),
                        pltpu.VMEM((nh, 1, 2 * bq), F32), pltpu.VMEM((nh, 1, 2 * bq), F32),
                        pltpu.VMEM((2 * nh, bk, 2 * bq), F32), pltpu.VMEM((2 * nh, bk, 2 * bq), BF16)],
        compiler_params=_cparams(("parallel", "parallel", "arbitrary")),
        name="attn_prompt",
    )(slopes, lamp, g_col, qt, k, vt, kn)


def _page_matrix(ref):
    return jnp.concatenate([ref[0, pl.ds(h, PAGE_SIZE, stride=N_HEADS), :].astype(BF16) for h in range(N_HEADS)],
                           axis=1)


def _sattn_kernel(lam_init, n_pages, pt_ref, lamp_ref, g_ref, q_ref, kn_ref, vn_ref, *rest):
    del pt_ref
    k_refs = rest[:n_pages]
    v_refs = rest[n_pages:2 * n_pages]
    o_ref = rest[2 * n_pages]
    acc_scr = rest[2 * n_pages + 1]
    t_new = q_ref.shape[1]
    n_past = n_pages * PAGE_SIZE
    nl = 2 * N_HEADS * t_new
    nt = (((1,), (1,)), ((), ()))
    tn = (((0,), (0,)), ((), ()))

    c = lax.broadcasted_iota(jnp.int32, (1, nl), 1)
    c_q = c & (t_new - 1)
    c_h = (c >> 3) & (N_HEADS - 1)
    c_m = c >> 6
    slope = jnp.exp2(-(c_h + 1).astype(F32))
    qpos = (n_past + c_q).astype(F32)

    r = lax.broadcasted_iota(jnp.int32, (nl, t_new), 0)
    sel = (r & (t_new - 1)) == lax.broadcasted_iota(jnp.int32, (nl, t_new), 1)
    rep = jnp.dot(sel.astype(BF16), q_ref[0], preferred_element_type=F32)
    rr = lax.broadcasted_iota(jnp.int32, (nl, ATT_WIDTH), 0)
    jj = lax.broadcasted_iota(jnp.int32, (nl, ATT_WIDTH), 1)
    same_head = ((rr >> 3) & (N_HEADS - 1)) == (jj >> 7)
    keep = same_head & ((rr >> 6) == ((jj >> 6) & 1))
    qbd = jnp.where(keep, rep, 0.0).astype(BF16)

    def rows_scaled(v):
        return jnp.concatenate([jnp.broadcast_to(v, (nl, nl)).T] * (ATT_WIDTH // nl), axis=1)

    dist = c_q - lax.broadcasted_iota(jnp.int32, (t_new, 1), 0)
    sn = lax.dot_general(kn_ref[0].astype(BF16), qbd, nt, preferred_element_type=F32)
    sn = jnp.where(dist >= 0, sn - slope * dist.astype(F32), NEG_INF)
    m = jnp.max(sn, axis=0, keepdims=True)
    pn = jnp.exp(sn - m)
    l = jnp.sum(pn, axis=0, keepdims=True)
    acc_scr[...] = lax.dot_general(pn.astype(BF16), vn_ref[0].astype(BF16), tn, preferred_element_type=F32)

    row = lax.broadcasted_iota(jnp.int32, (PAGE_SIZE, 1), 0).astype(F32)
    for i in range(n_pages):
        s = lax.dot_general(_page_matrix(k_refs[i]), qbd, nt, preferred_element_type=F32)
        s = s - slope * (qpos - (row + float(i * PAGE_SIZE)))
        m_new = jnp.maximum(m, jnp.max(s, axis=0, keepdims=True))
        alpha = jnp.exp(m - m_new)
        p = jnp.exp(s - m_new)
        l = alpha * l + jnp.sum(p, axis=0, keepdims=True)
        m = m_new
        pv = lax.dot_general(p.astype(BF16), _page_matrix(v_refs[i]), tn, preferred_element_type=F32)
        acc_scr[...] = acc_scr[...] * rows_scaled(alpha) + pv

    lam = _diff_lambda(lamp_ref[...], lam_init)
    coef = jnp.where(c_m == 0, 1.0, -lam) / l
    acc = jnp.where(same_head, acc_scr[...] * rows_scaled(coef), 0.0)
    o = jnp.sum(acc.reshape(nl // t_new, t_new, ATT_WIDTH), axis=0)
    for hh in range(N_HEADS):
        ls = slice(hh * V_DIM, (hh + 1) * V_DIM)
        o_ref[0, :, ls] = _head_rms(o[:, ls], g_ref[...], lam_init).astype(o_ref.dtype)


def _attn_sample(q, kn, vn, cache_k, cache_v, page_table, lamp, g, lam_init):
    nseq, t_new, _ = q.shape
    n_pages = page_table.shape[1]
    tok = pl.BlockSpec((1, t_new, ATT_WIDTH), lambda s, pt: (s, 0, 0))
    page = lambda i: pl.BlockSpec((1, PAGE_SIZE * N_HEADS, V_DIM), lambda s, pt: (pt[s, i], 0, 0))
    grid_spec = pltpu.PrefetchScalarGridSpec(
        num_scalar_prefetch=1,
        grid=(nseq,),
        in_specs=[pl.BlockSpec((4, HEAD_DIM), lambda s, pt: (0, 0)), pl.BlockSpec((1, V_DIM), lambda s, pt: (0, 0)),
                  tok, tok, tok]
                 + [page(i) for i in range(n_pages)] + [page(i) for i in range(n_pages)],
        out_specs=tok,
        scratch_shapes=[pltpu.VMEM((2 * N_HEADS * t_new, ATT_WIDTH), F32)],
    )
    return pl.pallas_call(
        functools.partial(_sattn_kernel, lam_init, n_pages),
        grid_spec=grid_spec,
        out_shape=jax.ShapeDtypeStruct((nseq, t_new, ATT_WIDTH), BF16),
        compiler_params=_cparams(("arbitrary",)),
        name="attn_sample",
    )(page_table, lamp, g, q, kn, vn, *([cache_k] * n_pages), *([cache_v] * n_pages))


MERGE_ROWS = 512


def _merge_kernel(x_ref, ssm_ref, att_ref, wg_ref, bg_ref, pssm_ref, patt_ref, wo_ref, g_ref, b_ref, o_ref):
    x = x_ref[...]
    gates = jax.nn.sigmoid(jnp.dot(x.astype(BF16), wg_ref[...], preferred_element_type=F32) + bg_ref[...])
    merged = (gates[:, :D_MODEL] * jnp.dot(ssm_ref[...], pssm_ref[...], preferred_element_type=F32)
              + gates[:, D_MODEL:] * jnp.dot(att_ref[...], patt_ref[...], preferred_element_type=F32))
    y = DN_ALPHA * x + jnp.dot(merged.astype(BF16), wo_ref[...], preferred_element_type=F32)
    o_ref[...] = _layer_norm(y, g_ref[...], b_ref[...])


def _merge(x2d, ssm_y, att, wg, bg, pssm, patt, wo, g, b, tm):
    t = x2d.shape[0]
    row = lambda n: pl.BlockSpec((tm, n), lambda i: (i, 0))
    return pl.pallas_call(
        _merge_kernel,
        grid=(t // tm,),
        in_specs=[row(D_MODEL), row(SSM_WIDTH), row(ATT_WIDTH),
                  _const_spec((D_MODEL, 2 * D_MODEL)), _const_spec((1, 2 * D_MODEL)),
                  _const_spec((SSM_WIDTH, D_MODEL)), _const_spec((ATT_WIDTH, D_MODEL)),
                  _const_spec((D_MODEL, D_MODEL)), _const_spec((1, D_MODEL)), _const_spec((1, D_MODEL))],
        out_specs=row(D_MODEL),
        out_shape=jax.ShapeDtypeStruct((t, D_MODEL), F32),
        compiler_params=_cparams(("parallel",)),
        name="merge",
    )(x2d, ssm_y, att, wg, bg, pssm, patt, wo, g, b)


def _route(x_bf16, wrg, brg, wre, bre):
    gl = jnp.dot(x_bf16, wrg, preferred_element_type=F32) + brg
    el = jnp.dot(x_bf16, wre, preferred_element_type=F32) + bre
    gi = lax.broadcasted_iota(jnp.int32, gl.shape, 1).astype(F32)
    gmax = jnp.max(gl, axis=-1, keepdims=True)
    g_sel = jnp.min(jnp.where(gl == gmax, gi, float(N_EXPERT_GROUPS)), axis=-1, keepdims=True)
    p_g = 1.0 / jnp.sum(jnp.exp(gl - gmax), axis=-1, keepdims=True)
    ei_int = lax.broadcasted_iota(jnp.int32, el.shape, 1)
    ei = ei_int.astype(F32)
    el = jnp.where((ei_int >> 3).astype(F32) == g_sel, el, -jnp.inf)
    v1 = jnp.max(el, axis=-1, keepdims=True)
    i1 = jnp.min(jnp.where(el == v1, ei, float(N_EXPERTS)), axis=-1, keepdims=True)
    el2 = jnp.where(ei == i1, -jnp.inf, el)
    v2 = jnp.max(el2, axis=-1, keepdims=True)
    i2 = jnp.min(jnp.where(el2 == v2, ei, float(N_EXPERTS)), axis=-1, keepdims=True)
    e2 = jnp.exp(v2 - v1)
    w1 = p_g / (1.0 + e2)
    w2 = p_g * e2 / (1.0 + e2)
    lane = lax.broadcasted_iota(jnp.int32, (gl.shape[0], LANES), 1).astype(F32)
    first = EXPERTS_PER_GROUP * g_sel
    comb = jnp.where(lane == i1 - first, w1, 0.0) + jnp.where(lane == i2 - first, w2, 0.0)
    return jnp.where(lane == g_sel, 1.0, 0.0), comb


MOE_ROWS = 128
MOE_EXPERTS_PER_STEP = 4


def _moe_kernel(x_ref, wrg_ref, brg_ref, wre_ref, bre_ref, wg_ref, wu_ref, wd_ref, g_ref, b_ref, o_ref,
                xs_scr, comb_scr, acc_scr, unsort_scr, seg_ref):
    e = pl.program_id(1)
    tm = x_ref.shape[0]
    srows = xs_scr.shape[0]
    lane = lax.broadcasted_iota(jnp.int32, (1, LANES), 1)

    @pl.when(e == 0)
    def _():
        xb = x_ref[...].astype(BF16)
        group_1h, comb = _route(xb, wrg_ref[...], brg_ref[...], wre_ref[...], bre_ref[...])
        tri = lax.broadcasted_iota(jnp.int32, (tm, tm), 0) >= lax.broadcasted_iota(jnp.int32, (tm, tm), 1)
        count = jnp.dot(jnp.where(tri, 1.0, 0.0).astype(BF16), group_1h.astype(BF16), preferred_element_type=F32)
        total = count[tm - 1:tm, :]
        padded = jnp.floor((total + (MOE_ROWS - 1)) * (1.0 / MOE_ROWS)) * MOE_ROWS
        start = jnp.zeros((1, LANES), F32)
        for gidx in range(N_EXPERT_GROUPS):
            size = jnp.sum(jnp.where(lane == gidx, padded, 0.0), axis=-1, keepdims=True)
            seg_ref[0, gidx] = jnp.sum(jnp.where(lane == gidx, start, 0.0)).astype(jnp.int32) // MOE_ROWS
            seg_ref[1, gidx] = jnp.sum(size).astype(jnp.int32) // MOE_ROWS
            start = start + jnp.where(lane > gidx, size, 0.0)
        pos = jnp.sum(group_1h * (start + count - 1.0), axis=-1, keepdims=True)
        unsort = jnp.where(lax.broadcasted_iota(jnp.int32, (tm, srows), 1).astype(F32) == pos, 1.0, 0.0)
        unsort_scr[...] = unsort.astype(BF16)
        pos_row = jnp.broadcast_to(pos, (tm, LANES)).T[0:1, :]
        sort = jnp.where(lax.broadcasted_iota(jnp.int32, (srows, tm), 0).astype(F32) == pos_row, 1.0, 0.0)
        sort = sort.astype(BF16)
        xs_scr[...] = jnp.dot(sort, xb, preferred_element_type=F32).astype(BF16)
        comb_hi = comb.astype(BF16)
        comb_lo = (comb - comb_hi.astype(F32)).astype(BF16)
        comb_s = jnp.dot(sort, jnp.concatenate([comb_hi, comb_lo], axis=1), preferred_element_type=F32)
        comb_scr[...] = comb_s[:, :LANES] + comb_s[:, LANES:]
        acc_scr[...] = jnp.zeros_like(acc_scr)

    first_expert = e * MOE_EXPERTS_PER_STEP
    grp = first_expert >> 3
    in_grp = first_expert & (EXPERTS_PER_GROUP - 1)
    first_blk = seg_ref[0, grp]

    def block(i):
        r0 = pl.multiple_of((first_blk + i) * MOE_ROWS, MOE_ROWS)
        xs = xs_scr[pl.ds(r0, MOE_ROWS), :]
        comb = comb_scr[pl.ds(r0, MOE_ROWS), :]
        out = None
        for k in range(MOE_EXPERTS_PER_STEP):
            hg = jnp.dot(xs, wg_ref[k], preferred_element_type=F32)
            hu = jnp.dot(xs, wu_ref[k], preferred_element_type=F32)
            ce = jnp.sum(jnp.where(lane == in_grp + k, comb, 0.0), axis=-1, keepdims=True)
            hid = (hg * jax.nn.sigmoid(hg)) * hu * ce
            part = jnp.dot(hid.astype(BF16), wd_ref[k], preferred_element_type=F32)
            out = part if out is None else out + part
        acc_scr[pl.ds(r0, MOE_ROWS), :] += out

    n_blk = seg_ref[1, grp]

    def pair(i, carry):
        block(2 * i)
        block(2 * i + 1)
        return carry

    lax.fori_loop(0, n_blk >> 1, pair, 0)

    @pl.when((n_blk & 1) != 0)
    def _():
        block(n_blk - 1)

    @pl.when(e == N_EXPERTS // MOE_EXPERTS_PER_STEP - 1)
    def _():
        ffn = jnp.dot(unsort_scr[...], acc_scr[...].astype(BF16), preferred_element_type=F32)
        o_ref[...] = _layer_norm(DN_ALPHA * x_ref[...] + ffn, g_ref[...], b_ref[...])


def _moe(x1, wrg, brg, wre, bre, wg, wu, wd, g, b, tm):
    t = x1.shape[0]
    srows = tm + N_EXPERT_GROUPS * MOE_ROWS
    row = pl.BlockSpec((tm, D_MODEL), lambda i, e: (i, 0))
    return pl.pallas_call(
        _moe_kernel,
        grid=(t // tm, N_EXPERTS // MOE_EXPERTS_PER_STEP),
        in_specs=[row,
                  _const_spec((D_MODEL, N_EXPERT_GROUPS)), _const_spec((1, N_EXPERT_GROUPS)),
                  _const_spec((D_MODEL, N_EXPERTS)), _const_spec((1, N_EXPERTS)),
                  pl.BlockSpec((MOE_EXPERTS_PER_STEP, D_MODEL, EXPERT_FF), lambda i, e: (e, 0, 0)),
                  pl.BlockSpec((MOE_EXPERTS_PER_STEP, D_MODEL, EXPERT_FF), lambda i, e: (e, 0, 0)),
                  pl.BlockSpec((MOE_EXPERTS_PER_STEP, EXPERT_FF, D_MODEL), lambda i, e: (e, 0, 0)),
                  _const_spec((1, D_MODEL)), _const_spec((1, D_MODEL))],
        out_specs=row,
        out_shape=jax.ShapeDtypeStruct((t, D_MODEL), F32),
        scratch_shapes=[pltpu.VMEM((srows, D_MODEL), BF16), pltpu.VMEM((srows, LANES), F32),
                        pltpu.VMEM((srows, D_MODEL), F32), pltpu.VMEM((tm, srows), BF16),
                        pltpu.SMEM((2, N_EXPERT_GROUPS), jnp.int32)],
        compiler_params=_cparams(("parallel", "arbitrary")),
        name="moe",
    )(x1, wrg, brg, wre, bre, wg, wu, wd, g, b)


class _Tiles(NamedTuple):
    proj: int
    ssm: int
    attn_q: int
    attn_k: int
    merge: int
    moe: int


def _tiles(seq, tokens):
    return _Tiles(proj=min(512, seq), ssm=min(512, seq), attn_q=min(ATTN_BQ, seq), attn_k=min(ATTN_BK, seq),
                  merge=min(MERGE_ROWS, tokens), moe=min(1024, tokens))


def kernel(x_prompt, x_sample, cache_k, cache_v, state_ssm_re, state_ssm_im, page_table, w_in, b_gate, ssm_lambda_re, ssm_lambda_im, ssm_log_dt, ssm_b_re, ssm_b_im, ssm_c_re, ssm_c_im, ssm_d, w_glu, b_glu, lambda_q1, lambda_k1, lambda_q2, lambda_k2, subln_g, p_ssm, p_att, w_o, ln1_g, ln1_b, w_router_group, b_router_group, w_router_expert, b_router_expert, w_exp_gate, w_exp_up, w_exp_down, ln2_g, ln2_b):
    bsz, seq, _ = x_prompt.shape
    nseq, t_new, _ = x_sample.shape
    assert w_in.shape[0] == DEPTH == 1 and t_new == SUBLANES
    l = 0
    lam_init = _lambda_init(l)
    row = lambda a: a[l].reshape(1, -1)

    w_qkv = w_in[l][:, :C_G].astype(BF16)
    w_gate = w_in[l][:, C_G:].astype(BF16)
    w_ukv = jnp.concatenate([w_qkv[:, :C_Q], w_qkv[:, C_K:]], axis=1)
    w_qv_t = jnp.concatenate([w_qkv[:, C_Q:C_K], w_qkv[:, C_V:]], axis=1).T
    ssm_tabs = _ssm_tables(ssm_lambda_re[l], ssm_lambda_im[l], ssm_log_dt[l], ssm_b_re[l], ssm_b_im[l],
                           ssm_c_re[l], ssm_c_im[l])
    bre, bim, cre, ncim, mult = ssm_tabs
    ssm_wts = (bre, bim, cre, ncim, row(ssm_d), w_glu[l].astype(BF16), row(b_glu), mult)
    lamp = jnp.stack([lambda_q1[l], lambda_k1[l], lambda_q2[l], lambda_k2[l]])
    g_sub = row(subln_g)
    slopes = jnp.exp2(-8.0 * (jnp.arange(N_HEADS, dtype=F32) + 1.0) / N_HEADS)
    merge_wts = (w_gate, row(b_gate), p_ssm[l].astype(BF16), p_att[l].astype(BF16), w_o[l].astype(BF16),
                 row(ln1_g), row(ln1_b))
    moe_wts = (w_router_group[l].astype(BF16), row(b_router_group), w_router_expert[l].astype(BF16),
               row(b_router_expert), w_exp_gate[l].astype(BF16), w_exp_up[l].astype(BF16),
               w_exp_down[l].astype(BF16), row(ln2_g), row(ln2_b))

    tp = bsz * seq
    xp = x_prompt.reshape(tp, D_MODEL)
    tl = _tiles(seq, tp)
    u, kf, kb, vf, qt, vt, kn = _project_prompt(xp, w_ukv, w_qv_t, tl.proj, tl.attn_q, tl.attn_k)
    ssm_y, srp, sip = _ssm_prompt(u.reshape(bsz, seq, SSM_WIDTH), ssm_wts, tl.ssm)
    att = _attn_prompt(qt, kb, vt, kn, slopes, lamp, subln_g[l].reshape(V_DIM, 1), lam_init, bsz, seq,
                       tl.attn_q, tl.attn_k, ATTN_HEADS)
    x1 = _merge(xp, ssm_y.reshape(tp, SSM_WIDTH), att.reshape(tp, ATT_WIDTH), *merge_wts, tl.merge)
    y_prompt = _moe(x1, *moe_wts, tl.moe).reshape(bsz, seq, D_MODEL)
    k_prompt = kf.reshape(1, bsz, seq, N_HEADS, V_DIM)
    v_prompt = vf.reshape(1, bsz, seq, N_HEADS, V_DIM)
    srp = srp.reshape(1, bsz, SSM_GROUPS, SSM_STATE)
    sip = sip.reshape(1, bsz, SSM_GROUPS, SSM_STATE)

    ts = nseq * t_new
    xs = x_sample.reshape(ts, D_MODEL)
    tl = _tiles(ts, ts)
    u, q, kf, kb, vf, vb = _project_rows(xs, w_qkv, tl.proj)
    ssm_y, srs, sis = _ssm_sample(u, state_ssm_re[l].reshape(nseq, N_STATE), state_ssm_im[l].reshape(nseq, N_STATE),
                                  ssm_wts, tl.ssm)
    att = _attn_sample(q.reshape(nseq, t_new, ATT_WIDTH), kb.reshape(nseq, t_new, ATT_WIDTH),
                       vb.reshape(nseq, t_new, ATT_WIDTH),
                       cache_k[l].reshape(-1, PAGE_SIZE * N_HEADS, V_DIM),
                       cache_v[l].reshape(-1, PAGE_SIZE * N_HEADS, V_DIM), page_table, lamp, g_sub, lam_init)
    x1 = _merge(xs, ssm_y, att.reshape(ts, ATT_WIDTH), *merge_wts, tl.merge)
    y_sample = _moe(x1, *moe_wts, tl.moe).reshape(nseq, t_new, D_MODEL)
    k_sample = kf.reshape(1, nseq, t_new, N_HEADS, V_DIM)
    v_sample = vf.reshape(1, nseq, t_new, N_HEADS, V_DIM)
    srs = srs.reshape(1, nseq, SSM_GROUPS, SSM_STATE)
    sis = sis.reshape(1, nseq, SSM_GROUPS, SSM_STATE)

    return (y_prompt, y_sample, k_prompt, v_prompt, srp, sip, k_sample, v_sample, srs, sis)
```

```python
import functools
import math
from typing import NamedTuple

import jax
import jax.numpy as jnp
from jax import lax
from jax.experimental import pallas as pl
from jax.experimental.pallas import tpu as pltpu

F32 = jnp.float32
BF16 = jnp.bfloat16

D_MODEL = 1024
N_HEADS = 8
HEAD_DIM = 64
V_DIM = 2 * HEAD_DIM
ATT_WIDTH = N_HEADS * V_DIM
SSM_WIDTH = 512
SSM_GROUP = 16
SSM_GROUPS = SSM_WIDTH // SSM_GROUP
SSM_STATE = 64
N_STATE = SSM_GROUPS * SSM_STATE
N_EXPERT_GROUPS = 4
EXPERTS_PER_GROUP = 8
N_EXPERTS = N_EXPERT_GROUPS * EXPERTS_PER_GROUP
EXPERT_FF = 256
PAGE_SIZE = 128
DEPTH = 1
DN_ALPHA = (2.0 * DEPTH) ** 0.25
LN_EPS = 1e-5
NEG_INF = -1e30
QK_SCALE = HEAD_DIM ** -0.5
LOG2E = math.log2(math.e)
C_Q = SSM_WIDTH
C_K = C_Q + ATT_WIDTH
C_V = C_K + ATT_WIDTH
C_G = C_V + ATT_WIDTH

VMEM_LIMIT_BYTES = 52 * 1024 * 1024
SUBLANES = 8
LANES = 128


def _cparams(sem):
    return pltpu.CompilerParams(dimension_semantics=sem, vmem_limit_bytes=VMEM_LIMIT_BYTES)


def _const_spec(shape):
    nd = len(shape)
    return pl.BlockSpec(shape, lambda *_: (0,) * nd)


def _layer_norm(x, g, b):
    mu = jnp.mean(x, axis=-1, keepdims=True)
    xc = x - mu
    var = jnp.mean(xc * xc, axis=-1, keepdims=True)
    return xc * lax.rsqrt(var + LN_EPS) * g + b


def _lambda_init(layer_idx):
    return 0.8 - 0.6 * math.exp(-0.3 * layer_idx)


def _diff_lambda(lamp, lam_init):
    a = jnp.sum(lamp[0:1, :] * lamp[1:2, :], axis=-1, keepdims=True)
    b = jnp.sum(lamp[2:3, :] * lamp[3:4, :], axis=-1, keepdims=True)
    return jnp.exp(a) - jnp.exp(b) + lam_init


def _store_heads(ref, val):
    n = val.shape[0]
    for h in range(N_HEADS):
        ref[pl.ds(h, n, stride=N_HEADS), :] = val[:, h * V_DIM:(h + 1) * V_DIM]


def _proj_rows_kernel(x_ref, w_ref, u_ref, q_ref, kf_ref, kb_ref, vf_ref, vb_ref):
    x = x_ref[...].astype(BF16)
    u_ref[...] = jnp.dot(x, w_ref[:, 0:C_Q], preferred_element_type=F32)
    q = jnp.dot(x, w_ref[:, C_Q:C_K], preferred_element_type=F32)
    q_ref[...] = (q * QK_SCALE).astype(BF16)
    k = jnp.dot(x, w_ref[:, C_K:C_V], preferred_element_type=F32)
    kb_ref[...] = k.astype(BF16)
    _store_heads(kf_ref, k)
    v = jnp.dot(x, w_ref[:, C_V:C_G], preferred_element_type=F32)
    vb_ref[...] = v.astype(BF16)
    _store_heads(vf_ref, v)


def _project_rows(x2d, w_bf16, tm):
    t = x2d.shape[0]
    row = lambda n: pl.BlockSpec((tm, n), lambda i: (i, 0))
    heads = pl.BlockSpec((tm * N_HEADS, V_DIM), lambda i: (i, 0))
    return pl.pallas_call(
        _proj_rows_kernel,
        grid=(t // tm,),
        in_specs=[row(D_MODEL), _const_spec((D_MODEL, C_G))],
        out_specs=[row(SSM_WIDTH), row(ATT_WIDTH), heads, row(ATT_WIDTH), heads, row(ATT_WIDTH)],
        out_shape=[
            jax.ShapeDtypeStruct((t, SSM_WIDTH), F32),
            jax.ShapeDtypeStruct((t, ATT_WIDTH), BF16),
            jax.ShapeDtypeStruct((t * N_HEADS, V_DIM), F32),
            jax.ShapeDtypeStruct((t, ATT_WIDTH), BF16),
            jax.ShapeDtypeStruct((t * N_HEADS, V_DIM), F32),
            jax.ShapeDtypeStruct((t, ATT_WIDTH), BF16),
        ],
        compiler_params=_cparams(("parallel",)),
        name="proj_rows",
    )(x2d, w_bf16)


def _proj_prompt_kernel(qblk, vblk, x_ref, w_ref, wt_ref, u_ref, kf_ref, kb_ref, vf_ref, qt_ref, vt_ref, kn_ref):
    x = x_ref[...].astype(BF16)
    u_ref[...] = jnp.dot(x, w_ref[:, 0:C_Q], preferred_element_type=F32)
    k = jnp.dot(x, w_ref[:, C_Q:C_Q + ATT_WIDTH], preferred_element_type=F32)
    kb = k.astype(BF16)
    kb_ref[...] = kb
    kr = kb.astype(F32)
    member = (lax.broadcasted_iota(jnp.int32, (ATT_WIDTH, LANES), 0) >> 6) == lax.broadcasted_iota(
        jnp.int32, (ATT_WIDTH, LANES), 1)
    ksq = jnp.dot((kr * kr).astype(BF16), jnp.where(member, 1.0, 0.0).astype(BF16), preferred_element_type=F32)
    kn_ref[0] = jnp.broadcast_to(jnp.max(ksq, axis=0, keepdims=True), (SUBLANES, LANES))
    _store_heads(kf_ref, k)
    _store_heads(vf_ref, jnp.dot(x, w_ref[:, C_Q + ATT_WIDTH:], preferred_element_type=F32))
    nt = (((1,), (1,)), ((), ()))
    qt = lax.dot_general(wt_ref[0:ATT_WIDTH, :], x, nt, preferred_element_type=F32)
    qt = (qt * (QK_SCALE * LOG2E)).astype(BF16)
    vt = lax.dot_general(wt_ref[ATT_WIDTH:, :], x, nt, preferred_element_type=F32).astype(BF16)
    for j in range(x.shape[0] // qblk):
        qt_ref[j] = qt[:, j * qblk:(j + 1) * qblk]
    for j in range(x.shape[0] // vblk):
        vt_ref[j] = vt[:, j * vblk:(j + 1) * vblk]


def _project_prompt(x2d, w_rows, w_cols, tm, qblk, vblk):
    t = x2d.shape[0]
    row = lambda n: pl.BlockSpec((tm, n), lambda i: (i, 0))
    heads = pl.BlockSpec((tm * N_HEADS, V_DIM), lambda i: (i, 0))
    tblk = lambda n: pl.BlockSpec((tm // n, ATT_WIDTH, n), lambda i: (i, 0, 0))
    return pl.pallas_call(
        functools.partial(_proj_prompt_kernel, qblk, vblk),
        grid=(t // tm,),
        in_specs=[row(D_MODEL), _const_spec(w_rows.shape), _const_spec(w_cols.shape)],
        out_specs=[row(SSM_WIDTH), heads, row(ATT_WIDTH), heads, tblk(qblk), tblk(vblk),
                   pl.BlockSpec((1, SUBLANES, LANES), lambda i: (i, 0, 0))],
        out_shape=[
            jax.ShapeDtypeStruct((t, SSM_WIDTH), F32),
            jax.ShapeDtypeStruct((t * N_HEADS, V_DIM), F32),
            jax.ShapeDtypeStruct((t, ATT_WIDTH), BF16),
            jax.ShapeDtypeStruct((t * N_HEADS, V_DIM), F32),
            jax.ShapeDtypeStruct((t // qblk, ATT_WIDTH, qblk), BF16),
            jax.ShapeDtypeStruct((t // vblk, ATT_WIDTH, vblk), BF16),
            jax.ShapeDtypeStruct((t // tm, SUBLANES, LANES), F32),
        ],
        compiler_params=_cparams(("parallel",)),
        name="proj_prompt",
    )(x2d, w_rows, w_cols)


def _ssm_tables(lam_re, lam_im, log_dt, b_re, b_im, c_re, c_im):
    dt = jnp.exp(log_dt)[:, None]
    mag = jnp.exp(lam_re * dt)
    a_re, a_im = mag * jnp.cos(lam_im * dt), mag * jnp.sin(lam_im * dt)
    den = lam_re * lam_re + lam_im * lam_im
    zr, zi = a_re - 1.0, a_im
    fr = (zr * lam_re + zi * lam_im) / den
    fi = (zi * lam_re - zr * lam_im) / den
    bb_re = fr[..., None] * b_re - fi[..., None] * b_im
    bb_im = fr[..., None] * b_im + fi[..., None] * b_re
    eye = jnp.eye(SSM_GROUPS, dtype=F32)
    blk_b = lambda m: jnp.einsum('gpc,gh->gchp', m, eye).reshape(SSM_WIDTH, N_STATE).astype(BF16)
    blk_c = lambda m: jnp.einsum('gcp,gh->gphc', m, eye).reshape(N_STATE, SSM_WIDTH).astype(BF16)
    ar, ai = a_re.reshape(1, N_STATE), a_im.reshape(1, N_STATE)

    def cmul(xr, xi, yr, yi):
        return xr * yr - xi * yi, xr * yi + xi * yr

    rows = lax.broadcasted_iota(jnp.int32, (SUBLANES, 1), 0)
    tabs = []
    pr, pi = ar, ai
    for k in range(3):
        keep = rows >= (1 << k)
        tabs += [jnp.where(keep, pr, 0.0), jnp.where(keep, pi, 0.0)]
        pr, pi = cmul(pr, pi, pr, pi)
    cr, ci = [ar], [ai]
    for _ in range(SUBLANES - 1):
        nr, ni = cmul(cr[-1], ci[-1], ar, ai)
        cr.append(nr)
        ci.append(ni)
    tabs += [jnp.concatenate(cr, axis=0), jnp.concatenate(ci, axis=0)]
    return blk_b(bb_re), blk_b(bb_im), blk_c(c_re), blk_c(-c_im), jnp.stack(tabs)


SCAN_LANES = 512


def _scan_rows(sre, sim, mult_ref, r0, carry_re, carry_im):
    out_re, out_im = [], []
    for c0 in range(0, N_STATE, SCAN_LANES):
        ls = slice(c0, c0 + SCAN_LANES)
        xr = sre[pl.ds(r0, SUBLANES), ls]
        xi = sim[pl.ds(r0, SUBLANES), ls]
        for k in range(3):
            d = 1 << k
            yr = pltpu.roll(xr, d, 0)
            yi = pltpu.roll(xi, d, 0)
            mr = mult_ref[2 * k, :, ls]
            mi = mult_ref[2 * k + 1, :, ls]
            xr, xi = xr + (mr * yr - mi * yi), xi + (mr * yi + mi * yr)
        pr = mult_ref[6, :, ls]
        pi = mult_ref[7, :, ls]
        cr = carry_re[:, ls]
        ci = carry_im[:, ls]
        xr, xi = xr + (pr * cr - pi * ci), xi + (pr * ci + pi * cr)
        sre[pl.ds(r0, SUBLANES), ls] = xr
        sim[pl.ds(r0, SUBLANES), ls] = xi
        out_re.append(xr[SUBLANES - 1:SUBLANES, :])
        out_im.append(xi[SUBLANES - 1:SUBLANES, :])
    return jnp.concatenate(out_re, axis=1), jnp.concatenate(out_im, axis=1)


def _gelu_tanh(x):
    return 0.5 * x * (1.0 + jnp.tanh(math.sqrt(2.0 / math.pi) * (x + 0.044715 * (x * x * x))))


SSM_HALVES = 2


def _ssm_halves():
    cw, sw = SSM_WIDTH // SSM_HALVES, N_STATE // SSM_HALVES
    return [(slice(i * cw, (i + 1) * cw), slice(i * sw, (i + 1) * sw)) for i in range(SSM_HALVES)]


def _ssm_head(u_ref, bre_ref, bim_ref, sre, sim):
    ub = u_ref[...].astype(BF16)
    for cs, ss in _ssm_halves():
        sre[:, ss] = jnp.dot(ub[:, cs], bre_ref[cs, ss], preferred_element_type=F32)
        sim[:, ss] = jnp.dot(ub[:, cs], bim_ref[cs, ss], preferred_element_type=F32)


def _ssm_tail(u_ref, cre_ref, ncim_ref, d_ref, wglu_ref, bglu_ref, y_ref, sre, sim):
    y = jnp.concatenate(
        [jnp.dot(sre[:, ss].astype(BF16), cre_ref[ss, cs], preferred_element_type=F32)
         + jnp.dot(sim[:, ss].astype(BF16), ncim_ref[ss, cs], preferred_element_type=F32)
         for cs, ss in _ssm_halves()], axis=1) + d_ref[...] * u_ref[...]
    y = _gelu_tanh(y)
    z = jnp.dot(y.astype(BF16), wglu_ref[...], preferred_element_type=F32) + bglu_ref[...]
    y_ref[...] = (y * jax.nn.sigmoid(z)).astype(BF16)


def _ssm_seq_kernel(u_ref, bre_ref, bim_ref, cre_ref, ncim_ref, d_ref, wglu_ref, bglu_ref, mult_ref,
                    y_ref, ore_ref, oim_ref, sre, sim, car_re, car_im):
    @pl.when(pl.program_id(1) == 0)
    def _():
        car_re[...] = jnp.zeros_like(car_re)
        car_im[...] = jnp.zeros_like(car_im)

    _ssm_head(u_ref, bre_ref, bim_ref, sre, sim)
    n_blk = u_ref.shape[0] // SUBLANES

    def body(n, carry):
        r0 = pl.multiple_of(n * SUBLANES, SUBLANES)
        return _scan_rows(sre, sim, mult_ref, r0, carry[0], carry[1])

    cr, ci = lax.fori_loop(0, n_blk, body, (car_re[...], car_im[...]))
    car_re[...] = cr
    car_im[...] = ci
    ore_ref[0] = cr
    oim_ref[0] = ci
    _ssm_tail(u_ref, cre_ref, ncim_ref, d_ref, wglu_ref, bglu_ref, y_ref, sre, sim)


def _ssm_dec_kernel(u_ref, s0re_ref, s0im_ref, bre_ref, bim_ref, cre_ref, ncim_ref, d_ref, wglu_ref, bglu_ref,
                    mult_ref, y_ref, ore_ref, oim_ref, sre, sim):
    _ssm_head(u_ref, bre_ref, bim_ref, sre, sim)
    n_blk = u_ref.shape[0] // SUBLANES

    def body(n, _):
        r0 = pl.multiple_of(n * SUBLANES, SUBLANES)
        cr, ci = _scan_rows(sre, sim, mult_ref, r0, s0re_ref[pl.ds(n, 1), :], s0im_ref[pl.ds(n, 1), :])
        ore_ref[pl.ds(n, 1), :] = cr
        oim_ref[pl.ds(n, 1), :] = ci
        return 0

    lax.fori_loop(0, n_blk, body, 0)
    _ssm_tail(u_ref, cre_ref, ncim_ref, d_ref, wglu_ref, bglu_ref, y_ref, sre, sim)


def _ssm_weight_specs():
    return [_const_spec((SSM_WIDTH, N_STATE)), _const_spec((SSM_WIDTH, N_STATE)),
            _const_spec((N_STATE, SSM_WIDTH)), _const_spec((N_STATE, SSM_WIDTH)),
            _const_spec((1, SSM_WIDTH)), _const_spec((SSM_WIDTH, SSM_WIDTH)), _const_spec((1, SSM_WIDTH)),
            _const_spec((8, SUBLANES, N_STATE))]


def _ssm_prompt(u3, wts, lc):
    bsz, seq, _ = u3.shape
    return pl.pallas_call(
        _ssm_seq_kernel,
        grid=(bsz, seq // lc),
        in_specs=[pl.BlockSpec((None, lc, SSM_WIDTH), lambda b, c: (b, c, 0))] + _ssm_weight_specs(),
        out_specs=[pl.BlockSpec((None, lc, SSM_WIDTH), lambda b, c: (b, c, 0)),
                   pl.BlockSpec((1, 1, N_STATE), lambda b, c: (b, 0, 0)),
                   pl.BlockSpec((1, 1, N_STATE), lambda b, c: (b, 0, 0))],
        out_shape=[jax.ShapeDtypeStruct((bsz, seq, SSM_WIDTH), BF16),
                   jax.ShapeDtypeStruct((bsz, 1, N_STATE), F32),
                   jax.ShapeDtypeStruct((bsz, 1, N_STATE), F32)],
        scratch_shapes=[pltpu.VMEM((lc, N_STATE), F32), pltpu.VMEM((lc, N_STATE), F32),
                        pltpu.VMEM((1, N_STATE), F32), pltpu.VMEM((1, N_STATE), F32)],
        compiler_params=_cparams(("parallel", "arbitrary")),
        name="ssm_prompt",
    )(u3, *wts)


def _ssm_sample(u2, s0_re, s0_im, wts, lc):
    t = u2.shape[0]
    nseq = lc // SUBLANES
    return pl.pallas_call(
        _ssm_dec_kernel,
        grid=(t // lc,),
        in_specs=[pl.BlockSpec((lc, SSM_WIDTH), lambda c: (c, 0)),
                  pl.BlockSpec((nseq, N_STATE), lambda c: (c, 0)),
                  pl.BlockSpec((nseq, N_STATE), lambda c: (c, 0))] + _ssm_weight_specs(),
        out_specs=[pl.BlockSpec((lc, SSM_WIDTH), lambda c: (c, 0)),
                   pl.BlockSpec((nseq, N_STATE), lambda c: (c, 0)),
                   pl.BlockSpec((nseq, N_STATE), lambda c: (c, 0))],
        out_shape=[jax.ShapeDtypeStruct((t, SSM_WIDTH), BF16),
                   jax.ShapeDtypeStruct((t // SUBLANES, N_STATE), F32),
                   jax.ShapeDtypeStruct((t // SUBLANES, N_STATE), F32)],
        scratch_shapes=[pltpu.VMEM((lc, N_STATE), F32), pltpu.VMEM((lc, N_STATE), F32)],
        compiler_params=_cparams(("parallel",)),
        name="ssm_sample",
    )(u2, s0_re, s0_im, *wts)


def _head_rms(o, g, lam_init):
    ms = jnp.mean(o * o, axis=-1, keepdims=True)
    return o * lax.rsqrt(ms + LN_EPS) * g * (1.0 - lam_init)


N_BIAS_TERMS = 3
ATTN_BQ = 256
ATTN_BK = 256
ATTN_HEADS = 4
ATTN_UNROLL = 4
UNDERFLOW_BITS = 160.0
NORM_SLACK = 1.05
ONES_ROWS = 16


def _attn_kernel(lam_init, bq, bk, nh, slopes_ref, lamp_ref, g_ref, qt_ref, k_ref, vt_ref, kn_ref, o_ref,
                 acc_ref, m_ref, a_ref, u_ref, p_ref):
    hg = pl.program_id(1)
    qi = pl.program_id(2)
    two = 2 * bq
    n_diag = bq // bk
    heads = range(nh)
    hs = lambda hh: slice(hh * V_DIM, (hh + 1) * V_DIM)

    slope2, w, bias_cols = [], [], []
    r = lax.broadcasted_iota(jnp.int32, (V_DIM, bq), 0)
    ones_rows = lax.broadcasted_iota(jnp.int32, (V_DIM, two), 0) < N_BIAS_TERMS
    c = lax.broadcasted_iota(jnp.int32, (bk, V_DIM), 0).astype(F32)
    lane = lax.broadcasted_iota(jnp.int32, (bk, V_DIM), 1)
    for hh in heads:
        slope2.append(slopes_ref[hg * nh + hh] * LOG2E)
        qt = qt_ref[0, hs(hh), :]
        zero = jnp.zeros_like(qt)
        w_top = jnp.concatenate([jnp.where(r < HEAD_DIM, qt, zero), jnp.where(r >= HEAD_DIM, qt, zero)], axis=1)
        w.append(jnp.concatenate([w_top, jnp.where(ones_rows, 1.0, 0.0).astype(BF16)], axis=0))
        rem = slope2[hh] * c
        cols = jnp.zeros((bk, V_DIM), F32)
        for t in range(N_BIAS_TERMS):
            term = rem.astype(BF16).astype(F32)
            cols = jnp.where(lane == t, term, cols)
            rem = rem - term
        bias_cols.append(cols.astype(BF16))

    ksq_all = jnp.max(jnp.max(kn_ref[...], axis=0), axis=0, keepdims=True)
    klane = lax.broadcasted_iota(jnp.int32, (1, LANES), 1) >> 1
    first_f = []
    for hh in heads:
        kn = jnp.sqrt(jnp.max(jnp.where(klane == hg * nh + hh, ksq_all, 0.0), axis=1, keepdims=True))
        qf = qt_ref[0, hs(hh), :].astype(F32)
        q2 = qf * qf
        qsq = jnp.maximum(jnp.sum(q2[:HEAD_DIM], axis=0, keepdims=True), jnp.sum(q2[HEAD_DIM:], axis=0, keepdims=True))
        qn = jnp.sqrt(jnp.max(qsq, axis=1, keepdims=True))
        reach = (NORM_SLACK * 2.0 * qn * kn + UNDERFLOW_BITS) / slope2[hh] + bk
        first_f.append(jnp.floor((jnp.full((1, 1), qi * bq, jnp.int32).astype(F32) - reach) * (1.0 / bk)) + 1.0)
    n_full = qi * n_diag

    def start_block(hds):
        f = functools.reduce(jnp.minimum, [first_f[hh] for hh in hds])
        return jnp.minimum(jnp.clip(f, 0.0, 1e9).astype(jnp.int32)[0, 0] & -2, n_full & -2)

    near_heads = list(heads)[:max(nh // 2, 1)]
    far_heads = list(heads)[len(near_heads):]
    first_near = start_block(near_heads)
    first_all = jnp.minimum(start_block(list(heads)), first_near)

    ones_rows16 = jnp.ones((ONES_ROWS, bk), BF16)

    m_ref[...] = jnp.full(m_ref.shape, NEG_INF, F32)
    a_ref[...] = jnp.ones(a_ref.shape, F32)
    acc_ref[...] = jnp.zeros(acc_ref.shape, F32)
    for hh in heads:
        p_ref[2 * hh + 1] = jnp.zeros((bk, two), BF16)

    def scores(hh, kj, slot):
        ks = pl.multiple_of(kj * bk, bk)
        kaug = jnp.concatenate([k_ref[pl.ds(ks, bk), hs(hh)], bias_cols[hh]], axis=1)
        u_ref[2 * hh + slot] = jnp.dot(kaug, w[hh], preferred_element_type=F32)

    def softmax(hh, kj, slot, diag):
        u = u_ref[2 * hh + slot]
        if diag is not None:
            keep = (lax.broadcasted_iota(jnp.int32, (bk, two), 1) & (bq - 1)) >= (
                lax.broadcasted_iota(jnp.int32, (bk, two), 0) + diag * bk)
            u = jnp.where(keep, u, NEG_INF)
        off = jnp.full((1, 1), kj * bk - qi * bq, jnp.int32).astype(F32) * slope2[hh]
        m_old = m_ref[hh]
        m_new = jnp.maximum(m_old, jnp.max(u, axis=0, keepdims=True) + off)
        a_ref[hh] = jnp.exp2(m_old - m_new)
        m_ref[hh] = m_new
        p_ref[2 * hh + slot] = jnp.exp2(u + (off - m_new)).astype(BF16)

    def values(hh, kj, slot):
        vaug = jnp.concatenate([vt_ref[jnp.maximum(kj, 0), hs(hh), :], ones_rows16], axis=0)
        acc_ref[hh] = a_ref[hh] * acc_ref[hh] + jnp.dot(vaug, p_ref[2 * hh + slot], preferred_element_type=F32)

    def run(kj, slot, hds):
        for hh in hds:
            values(hh, kj - 1, 1 - slot)
            softmax(hh, kj, slot, None)
            scores(hh, kj + 1, 1 - slot)

    def span(start, n_run, hds, odd_possible):
        def body(i, carry):
            for d in range(ATTN_UNROLL):
                run(start + i * ATTN_UNROLL + d, d & 1, hds)
            return carry

        n_main = n_run // ATTN_UNROLL
        lax.fori_loop(0, n_main, body, 0)
        base = start + n_main * ATTN_UNROLL
        arm = ATTN_UNROLL // 2
        while arm >= 1:
            if arm > 1 or odd_possible:
                @pl.when((n_run & arm) != 0)
                def _(base=base, arm=arm):
                    for d in range(arm):
                        run(base + d, d & 1, hds)
                base = base + (n_run & arm)
            arm //= 2

    for hh in far_heads:
        scores(hh, first_all, 0)
    if far_heads:
        span(first_all, first_near - first_all, far_heads, False)
    for hh in near_heads:
        scores(hh, first_near, 0)
    span(first_near, n_full - first_near, list(heads), n_diag % 2 != 0)
    for d in range(n_diag):
        slot = d & 1 if n_diag % 2 == 0 else (n_full + d) & 1
        for hh in heads:
            values(hh, n_full + d - 1, 1 - slot)
            softmax(hh, n_full + d, slot, d)
            if d + 1 < n_diag:
                scores(hh, n_full + d + 1, 1 - slot)
    lam = _diff_lambda(lamp_ref[...], lam_init)
    for hh in heads:
        values(hh, n_full + n_diag - 1, slot)
        acc = acc_ref[hh]
        inv = 1.0 / acc[V_DIM:V_DIM + 1, :]
        acc = acc[:V_DIM, :]
        ot = acc[:, :bq] * inv[:, :bq] - lam * (acc[:, bq:] * inv[:, bq:])
        ms = jnp.mean(ot * ot, axis=0, keepdims=True)
        ot = ot * lax.rsqrt(ms + LN_EPS) * (g_ref[...] * (1.0 - lam_init))
        o_ref[0, :, hs(hh)] = ot.T.astype(o_ref.dtype)


def _attn_prompt(qt, k, vt, kn, slopes, lamp, g_col, lam_init, bsz, seq, bq, bk, nh):
    nq, nk = seq // bq, seq // bk
    hw = nh * V_DIM
    kn_tiles = kn.shape[0] // bsz
    return pl.pallas_call(
        functools.partial(_attn_kernel, lam_init, bq, bk, nh),
        grid=(bsz, N_HEADS // nh, nq),
        in_specs=[pl.BlockSpec(memory_space=pltpu.SMEM),
                  _const_spec((4, HEAD_DIM)), _const_spec((V_DIM, 1)),
                  pl.BlockSpec((1, hw, bq), lambda b, h, i: (b * nq + i, h, 0)),
                  pl.BlockSpec((seq, hw), lambda b, h, i: (b, h)),
                  pl.BlockSpec((nk, hw, bk), lambda b, h, i: (b, h, 0)),
                  pl.BlockSpec((kn_tiles, SUBLANES, LANES), lambda b, h, i: (b, 0, 0))],
        out_specs=pl.BlockSpec((1, bq, hw), lambda b, h, i: (b, i, h)),
        out_shape=jax.ShapeDtypeStruct((bsz, seq, ATT_WIDTH), BF16),
        scratch_shapes=[pltpu.VMEM((nh, V_DIM + ONES_ROWS, 2 * bq), F32),
                        pltpu.VMEM((nh, 1, 2 * bq), F32), pltpu.VMEM((nh, 1, 2 * bq), F32),
                        pltpu.VMEM((2 * nh, bk, 2 * bq), F32), pltpu.VMEM((2 * nh, bk, 2 * bq), BF16)],
        compiler_params=_cparams(("parallel", "parallel", "arbitrary")),
        name="attn_prompt",
    )(slopes, lamp, g_col, qt, k, vt, kn)


def _page_matrix(ref):
    return jnp.concatenate([ref[0, pl.ds(h, PAGE_SIZE, stride=N_HEADS), :].astype(BF16) for h in range(N_HEADS)],
                           axis=1)


def _sattn_kernel(lam_init, n_pages, pt_ref, lamp_ref, g_ref, q_ref, kn_ref, vn_ref, *rest):
    del pt_ref
    k_refs = rest[:n_pages]
    v_refs = rest[n_pages:2 * n_pages]
    o_ref = rest[2 * n_pages]
    acc_scr = rest[2 * n_pages + 1]
    t_new = q_ref.shape[1]
    n_past = n_pages * PAGE_SIZE
    nl = 2 * N_HEADS * t_new
    nt = (((1,), (1,)), ((), ()))
    tn = (((0,), (0,)), ((), ()))

    c = lax.broadcasted_iota(jnp.int32, (1, nl), 1)
    c_q = c & (t_new - 1)
    c_h = (c >> 3) & (N_HEADS - 1)
    c_m = c >> 6
    slope = jnp.exp2(-(c_h + 1).astype(F32))
    qpos = (n_past + c_q).astype(F32)

    r = lax.broadcasted_iota(jnp.int32, (nl, t_new), 0)
    sel = (r & (t_new - 1)) == lax.broadcasted_iota(jnp.int32, (nl, t_new), 1)
    rep = jnp.dot(sel.astype(BF16), q_ref[0], preferred_element_type=F32)
    rr = lax.broadcasted_iota(jnp.int32, (nl, ATT_WIDTH), 0)
    jj = lax.broadcasted_iota(jnp.int32, (nl, ATT_WIDTH), 1)
    same_head = ((rr >> 3) & (N_HEADS - 1)) == (jj >> 7)
    keep = same_head & ((rr >> 6) == ((jj >> 6) & 1))
    qbd = jnp.where(keep, rep, 0.0).astype(BF16)

    def rows_scaled(v):
        return jnp.concatenate([jnp.broadcast_to(v, (nl, nl)).T] * (ATT_WIDTH // nl), axis=1)

    dist = c_q - lax.broadcasted_iota(jnp.int32, (t_new, 1), 0)
    sn = lax.dot_general(kn_ref[0].astype(BF16), qbd, nt, preferred_element_type=F32)
    sn = jnp.where(dist >= 0, sn - slope * dist.astype(F32), NEG_INF)
    m = jnp.max(sn, axis=0, keepdims=True)
    pn = jnp.exp(sn - m)
    l = jnp.sum(pn, axis=0, keepdims=True)
    acc_scr[...] = lax.dot_general(pn.astype(BF16), vn_ref[0].astype(BF16), tn, preferred_element_type=F32)

    row = lax.broadcasted_iota(jnp.int32, (PAGE_SIZE, 1), 0).astype(F32)
    for i in range(n_pages):
        s = lax.dot_general(_page_matrix(k_refs[i]), qbd, nt, preferred_element_type=F32)
        s = s - slope * (qpos - (row + float(i * PAGE_SIZE)))
        m_new = jnp.maximum(m, jnp.max(s, axis=0, keepdims=True))
        alpha = jnp.exp(m - m_new)
        p = jnp.exp(s - m_new)
        l = alpha * l + jnp.sum(p, axis=0, keepdims=True)
        m = m_new
        pv = lax.dot_general(p.astype(BF16), _page_matrix(v_refs[i]), tn, preferred_element_type=F32)
        acc_scr[...] = acc_scr[...] * rows_scaled(alpha) + pv

    lam = _diff_lambda(lamp_ref[...], lam_init)
    coef = jnp.where(c_m == 0, 1.0, -lam) / l
    acc = jnp.where(same_head, acc_scr[...] * rows_scaled(coef), 0.0)
    o = jnp.sum(acc.reshape(nl // t_new, t_new, ATT_WIDTH), axis=0)
    for hh in range(N_HEADS):
        ls = slice(hh * V_DIM, (hh + 1) * V_DIM)
        o_ref[0, :, ls] = _head_rms(o[:, ls], g_ref[...], lam_init).astype(o_ref.dtype)


def _attn_sample(q, kn, vn, cache_k, cache_v, page_table, lamp, g, lam_init):
    nseq, t_new, _ = q.shape
    n_pages = page_table.shape[1]
    tok = pl.BlockSpec((1, t_new, ATT_WIDTH), lambda s, pt: (s, 0, 0))
    page = lambda i: pl.BlockSpec((1, PAGE_SIZE * N_HEADS, V_DIM), lambda s, pt: (pt[s, i], 0, 0))
    grid_spec = pltpu.PrefetchScalarGridSpec(
        num_scalar_prefetch=1,
        grid=(nseq,),
        in_specs=[pl.BlockSpec((4, HEAD_DIM), lambda s, pt: (0, 0)), pl.BlockSpec((1, V_DIM), lambda s, pt: (0, 0)),
                  tok, tok, tok]
                 + [page(i) for i in range(n_pages)] + [page(i) for i in range(n_pages)],
        out_specs=tok,
        scratch_shapes=[pltpu.VMEM((2 * N_HEADS * t_new, ATT_WIDTH), F32)],
    )
    return pl.pallas_call(
        functools.partial(_sattn_kernel, lam_init, n_pages),
        grid_spec=grid_spec,
        out_shape=jax.ShapeDtypeStruct((nseq, t_new, ATT_WIDTH), BF16),
        compiler_params=_cparams(("arbitrary",)),
        name="attn_sample",
    )(page_table, lamp, g, q, kn, vn, *([cache_k] * n_pages), *([cache_v] * n_pages))


MERGE_ROWS = 512


def _merge_kernel(x_ref, ssm_ref, att_ref, wg_ref, bg_ref, pssm_ref, patt_ref, wo_ref, g_ref, b_ref, o_ref):
    x = x_ref[...]
    gates = jax.nn.sigmoid(jnp.dot(x.astype(BF16), wg_ref[...], preferred_element_type=F32) + bg_ref[...])
    merged = (gates[:, :D_MODEL] * jnp.dot(ssm_ref[...], pssm_ref[...], preferred_element_type=F32)
              + gates[:, D_MODEL:] * jnp.dot(att_ref[...], patt_ref[...], preferred_element_type=F32))
    y = DN_ALPHA * x + jnp.dot(merged.astype(BF16), wo_ref[...], preferred_element_type=F32)
    o_ref[...] = _layer_norm(y, g_ref[...], b_ref[...])


def _merge(x2d, ssm_y, att, wg, bg, pssm, patt, wo, g, b, tm):
    t = x2d.shape[0]
    row = lambda n: pl.BlockSpec((tm, n), lambda i: (i, 0))
    return pl.pallas_call(
        _merge_kernel,
        grid=(t // tm,),
        in_specs=[row(D_MODEL), row(SSM_WIDTH), row(ATT_WIDTH),
                  _const_spec((D_MODEL, 2 * D_MODEL)), _const_spec((1, 2 * D_MODEL)),
                  _const_spec((SSM_WIDTH, D_MODEL)), _const_spec((ATT_WIDTH, D_MODEL)),
                  _const_spec((D_MODEL, D_MODEL)), _const_spec((1, D_MODEL)), _const_spec((1, D_MODEL))],
        out_specs=row(D_MODEL),
        out_shape=jax.ShapeDtypeStruct((t, D_MODEL), F32),
        compiler_params=_cparams(("parallel",)),
        name="merge",
    )(x2d, ssm_y, att, wg, bg, pssm, patt, wo, g, b)


def _route(x_bf16, wrg, brg, wre, bre):
    gl = jnp.dot(x_bf16, wrg, preferred_element_type=F32) + brg
    el = jnp.dot(x_bf16, wre, preferred_element_type=F32) + bre
    gi = lax.broadcasted_iota(jnp.int32, gl.shape, 1).astype(F32)
    gmax = jnp.max(gl, axis=-1, keepdims=True)
    g_sel = jnp.min(jnp.where(gl == gmax, gi, float(N_EXPERT_GROUPS)), axis=-1, keepdims=True)
    p_g = 1.0 / jnp.sum(jnp.exp(gl - gmax), axis=-1, keepdims=True)
    ei_int = lax.broadcasted_iota(jnp.int32, el.shape, 1)
    ei = ei_int.astype(F32)
    el = jnp.where((ei_int >> 3).astype(F32) == g_sel, el, -jnp.inf)
    v1 = jnp.max(el, axis=-1, keepdims=True)
    i1 = jnp.min(jnp.where(el == v1, ei, float(N_EXPERTS)), axis=-1, keepdims=True)
    el2 = jnp.where(ei == i1, -jnp.inf, el)
    v2 = jnp.max(el2, axis=-1, keepdims=True)
    i2 = jnp.min(jnp.where(el2 == v2, ei, float(N_EXPERTS)), axis=-1, keepdims=True)
    e2 = jnp.exp(v2 - v1)
    w1 = p_g / (1.0 + e2)
    w2 = p_g * e2 / (1.0 + e2)
    lane = lax.broadcasted_iota(jnp.int32, (gl.shape[0], LANES), 1).astype(F32)
    first = EXPERTS_PER_GROUP * g_sel
    comb = jnp.where(lane == i1 - first, w1, 0.0) + jnp.where(lane == i2 - first, w2, 0.0)
    return jnp.where(lane == g_sel, 1.0, 0.0), comb


MOE_ROWS = 128
MOE_EXPERTS_PER_STEP = 4


def _moe_kernel(x_ref, wrg_ref, brg_ref, wre_ref, bre_ref, wg_ref, wu_ref, wd_ref, g_ref, b_ref, o_ref,
                xs_scr, comb_scr, acc_scr, unsort_scr, seg_ref):
    e = pl.program_id(1)
    tm = x_ref.shape[0]
    srows = xs_scr.shape[0]
    lane = lax.broadcasted_iota(jnp.int32, (1, LANES), 1)

    @pl.when(e == 0)
    def _():
        xb = x_ref[...].astype(BF16)
        group_1h, comb = _route(xb, wrg_ref[...], brg_ref[...], wre_ref[...], bre_ref[...])
        tri = lax.broadcasted_iota(jnp.int32, (tm, tm), 0) >= lax.broadcasted_iota(jnp.int32, (tm, tm), 1)
        count = jnp.dot(jnp.where(tri, 1.0, 0.0).astype(BF16), group_1h.astype(BF16), preferred_element_type=F32)
        total = count[tm - 1:tm, :]
        padded = jnp.floor((total + (MOE_ROWS - 1)) * (1.0 / MOE_ROWS)) * MOE_ROWS
        start = jnp.zeros((1, LANES), F32)
        for gidx in range(N_EXPERT_GROUPS):
            size = jnp.sum(jnp.where(lane == gidx, padded, 0.0), axis=-1, keepdims=True)
            seg_ref[0, gidx] = jnp.sum(jnp.where(lane == gidx, start, 0.0)).astype(jnp.int32) // MOE_ROWS
            seg_ref[1, gidx] = jnp.sum(size).astype(jnp.int32) // MOE_ROWS
            start = start + jnp.where(lane > gidx, size, 0.0)
        pos = jnp.sum(group_1h * (start + count - 1.0), axis=-1, keepdims=True)
        unsort = jnp.where(lax.broadcasted_iota(jnp.int32, (tm, srows), 1).astype(F32) == pos, 1.0, 0.0)
        unsort_scr[...] = unsort.astype(BF16)
        pos_row = jnp.broadcast_to(pos, (tm, LANES)).T[0:1, :]
        sort = jnp.where(lax.broadcasted_iota(jnp.int32, (srows, tm), 0).astype(F32) == pos_row, 1.0, 0.0)
        sort = sort.astype(BF16)
        xs_scr[...] = jnp.dot(sort, xb, preferred_element_type=F32).astype(BF16)
        comb_hi = comb.astype(BF16)
        comb_lo = (comb - comb_hi.astype(F32)).astype(BF16)
        comb_s = jnp.dot(sort, jnp.concatenate([comb_hi, comb_lo], axis=1), preferred_element_type=F32)
        comb_scr[...] = comb_s[:, :LANES] + comb_s[:, LANES:]
        acc_scr[...] = jnp.zeros_like(acc_scr)

    first_expert = e * MOE_EXPERTS_PER_STEP
    grp = first_expert >> 3
    in_grp = first_expert & (EXPERTS_PER_GROUP - 1)
    first_blk = seg_ref[0, grp]

    def block(i):
        r0 = pl.multiple_of((first_blk + i) * MOE_ROWS, MOE_ROWS)
        xs = xs_scr[pl.ds(r0, MOE_ROWS), :]
        comb = comb_scr[pl.ds(r0, MOE_ROWS), :]
        out = None
        for k in range(MOE_EXPERTS_PER_STEP):
            hg = jnp.dot(xs, wg_ref[k], preferred_element_type=F32)
            hu = jnp.dot(xs, wu_ref[k], preferred_element_type=F32)
            ce = jnp.sum(jnp.where(lane == in_grp + k, comb, 0.0), axis=-1, keepdims=True)
            hid = (hg * jax.nn.sigmoid(hg)) * hu * ce
            part = jnp.dot(hid.astype(BF16), wd_ref[k], preferred_element_type=F32)
            out = part if out is None else out + part
        acc_scr[pl.ds(r0, MOE_ROWS), :] += out

    n_blk = seg_ref[1, grp]

    def pair(i, carry):
        block(2 * i)
        block(2 * i + 1)
        return carry

    lax.fori_loop(0, n_blk >> 1, pair, 0)

    @pl.when((n_blk & 1) != 0)
    def _():
        block(n_blk - 1)

    @pl.when(e == N_EXPERTS // MOE_EXPERTS_PER_STEP - 1)
    def _():
        ffn = jnp.dot(unsort_scr[...], acc_scr[...].astype(BF16), preferred_element_type=F32)
        o_ref[...] = _layer_norm(DN_ALPHA * x_ref[...] + ffn, g_ref[...], b_ref[...])


def _moe(x1, wrg, brg, wre, bre, wg, wu, wd, g, b, tm):
    t = x1.shape[0]
    srows = tm + N_EXPERT_GROUPS * MOE_ROWS
    row = pl.BlockSpec((tm, D_MODEL), lambda i, e: (i, 0))
    return pl.pallas_call(
        _moe_kernel,
        grid=(t // tm, N_EXPERTS // MOE_EXPERTS_PER_STEP),
        in_specs=[row,
                  _const_spec((D_MODEL, N_EXPERT_GROUPS)), _const_spec((1, N_EXPERT_GROUPS)),
                  _const_spec((D_MODEL, N_EXPERTS)), _const_spec((1, N_EXPERTS)),
                  pl.BlockSpec((MOE_EXPERTS_PER_STEP, D_MODEL, EXPERT_FF), lambda i, e: (e, 0, 0)),
                  pl.BlockSpec((MOE_EXPERTS_PER_STEP, D_MODEL, EXPERT_FF), lambda i, e: (e, 0, 0)),
                  pl.BlockSpec((MOE_EXPERTS_PER_STEP, EXPERT_FF, D_MODEL), lambda i, e: (e, 0, 0)),
                  _const_spec((1, D_MODEL)), _const_spec((1, D_MODEL))],
        out_specs=row,
        out_shape=jax.ShapeDtypeStruct((t, D_MODEL), F32),
        scratch_shapes=[pltpu.VMEM((srows, D_MODEL), BF16), pltpu.VMEM((srows, LANES), F32),
                        pltpu.VMEM((srows, D_MODEL), F32), pltpu.VMEM((tm, srows), BF16),
                        pltpu.SMEM((2, N_EXPERT_GROUPS), jnp.int32)],
        compiler_params=_cparams(("parallel", "arbitrary")),
        name="moe",
    )(x1, wrg, brg, wre, bre, wg, wu, wd, g, b)


class _Tiles(NamedTuple):
    proj: int
    ssm: int
    attn_q: int
    attn_k: int
    merge: int
    moe: int


def _tiles(seq, tokens):
    return _Tiles(proj=min(512, seq), ssm=min(512, seq), attn_q=min(ATTN_BQ, seq), attn_k=min(ATTN_BK, seq),
                  merge=min(MERGE_ROWS, tokens), moe=min(1024, tokens))


def kernel(x_prompt, x_sample, cache_k, cache_v, state_ssm_re, state_ssm_im, page_table, w_in, b_gate, ssm_lambda_re, ssm_lambda_im, ssm_log_dt, ssm_b_re, ssm_b_im, ssm_c_re, ssm_c_im, ssm_d, w_glu, b_glu, lambda_q1, lambda_k1, lambda_q2, lambda_k2, subln_g, p_ssm, p_att, w_o, ln1_g, ln1_b, w_router_group, b_router_group, w_router_expert, b_router_expert, w_exp_gate, w_exp_up, w_exp_down, ln2_g, ln2_b):
    bsz, seq, _ = x_prompt.shape
    nseq, t_new, _ = x_sample.shape
    assert w_in.shape[0] == DEPTH == 1 and t_new == SUBLANES
    l = 0
    lam_init = _lambda_init(l)
    row = lambda a: a[l].reshape(1, -1)

    w_qkv = w_in[l][:, :C_G].astype(BF16)
    w_gate = w_in[l][:, C_G:].astype(BF16)
    w_ukv = jnp.concatenate([w_qkv[:, :C_Q], w_qkv[:, C_K:]], axis=1)
    w_qv_t = jnp.concatenate([w_qkv[:, C_Q:C_K], w_qkv[:, C_V:]], axis=1).T
    ssm_tabs = _ssm_tables(ssm_lambda_re[l], ssm_lambda_im[l], ssm_log_dt[l], ssm_b_re[l], ssm_b_im[l],
                           ssm_c_re[l], ssm_c_im[l])
    bre, bim, cre, ncim, mult = ssm_tabs
    ssm_wts = (bre, bim, cre, ncim, row(ssm_d), w_glu[l].astype(BF16), row(b_glu), mult)
    lamp = jnp.stack([lambda_q1[l], lambda_k1[l], lambda_q2[l], lambda_k2[l]])
    g_sub = row(subln_g)
    slopes = jnp.exp2(-8.0 * (jnp.arange(N_HEADS, dtype=F32) + 1.0) / N_HEADS)
    merge_wts = (w_gate, row(b_gate), p_ssm[l].astype(BF16), p_att[l].astype(BF16), w_o[l].astype(BF16),
                 row(ln1_g), row(ln1_b))
    moe_wts = (w_router_group[l].astype(BF16), row(b_router_group), w_router_expert[l].astype(BF16),
               row(b_router_expert), w_exp_gate[l].astype(BF16), w_exp_up[l].astype(BF16),
               w_exp_down[l].astype(BF16), row(ln2_g), row(ln2_b))

    tp = bsz * seq
    xp = x_prompt.reshape(tp, D_MODEL)
    tl = _tiles(seq, tp)
    u, kf, kb, vf, qt, vt, kn = _project_prompt(xp, w_ukv, w_qv_t, tl.proj, tl.attn_q, tl.attn_k)
    ssm_y, srp, sip = _ssm_prompt(u.reshape(bsz, seq, SSM_WIDTH), ssm_wts, tl.ssm)
    att = _attn_prompt(qt, kb, vt, kn, slopes, lamp, subln_g[l].reshape(V_DIM, 1), lam_init, bsz, seq,
                       tl.attn_q, tl.attn_k, ATTN_HEADS)
    x1 = _merge(xp, ssm_y.reshape(tp, SSM_WIDTH), att.reshape(tp, ATT_WIDTH), *merge_wts, tl.merge)
    y_prompt = _moe(x1, *moe_wts, tl.moe).reshape(bsz, seq, D_MODEL)
    k_prompt = kf.reshape(1, bsz, seq, N_HEADS, V_DIM)
    v_prompt = vf.reshape(1, bsz, seq, N_HEADS, V_DIM)
    srp = srp.reshape(1, bsz, SSM_GROUPS, SSM_STATE)
    sip = sip.reshape(1, bsz, SSM_GROUPS, SSM_STATE)

    ts = nseq * t_new
    xs = x_sample.reshape(ts, D_MODEL)
    tl = _tiles(ts, ts)
    u, q, kf, kb, vf, vb = _project_rows(xs, w_qkv, tl.proj)
    ssm_y, srs, sis = _ssm_sample(u, state_ssm_re[l].reshape(nseq, N_STATE), state_ssm_im[l].reshape(nseq, N_STATE),
                                  ssm_wts, tl.ssm)
    att = _attn_sample(q.reshape(nseq, t_new, ATT_WIDTH), kb.reshape(nseq, t_new, ATT_WIDTH),
                       vb.reshape(nseq, t_new, ATT_WIDTH),
                       cache_k[l].reshape(-1, PAGE_SIZE * N_HEADS, V_DIM),
                       cache_v[l].reshape(-1, PAGE_SIZE * N_HEADS, V_DIM), page_table, lamp, g_sub, lam_init)
    x1 = _merge(xs, ssm_y, att.reshape(ts, ATT_WIDTH), *merge_wts, tl.merge)
    y_sample = _moe(x1, *moe_wts, tl.moe).reshape(nseq, t_new, D_MODEL)
    k_sample = kf.reshape(1, nseq, t_new, N_HEADS, V_DIM)
    v_sample = vf.reshape(1, nseq, t_new, N_HEADS, V_DIM)
    srs = srs.reshape(1, nseq, SSM_GROUPS, SSM_STATE)
    sis = sis.reshape(1, nseq, SSM_GROUPS, SSM_STATE)

    return (y_prompt, y_sample, k_prompt, v_prompt, srp, sip, k_sample, v_sample, srs, sis)
```

```python
import functools
import math
from typing import NamedTuple

import jax
import jax.numpy as jnp
from jax import lax
from jax.experimental import pallas as pl
from jax.experimental.pallas import tpu as pltpu

F32 = jnp.float32
BF16 = jnp.bfloat16

D_MODEL = 1024
N_HEADS = 8
HEAD_DIM = 64
V_DIM = 2 * HEAD_DIM
ATT_WIDTH = N_HEADS * V_DIM
SSM_WIDTH = 512
SSM_GROUP = 16
SSM_GROUPS = SSM_WIDTH // SSM_GROUP
SSM_STATE = 64
N_STATE = SSM_GROUPS * SSM_STATE
N_EXPERT_GROUPS = 4
EXPERTS_PER_GROUP = 8
N_EXPERTS = N_EXPERT_GROUPS * EXPERTS_PER_GROUP
EXPERT_FF = 256
PAGE_SIZE = 128
DEPTH = 1
DN_ALPHA = (2.0 * DEPTH) ** 0.25
LN_EPS = 1e-5
NEG_INF = -1e30
QK_SCALE = HEAD_DIM ** -0.5
LOG2E = math.log2(math.e)
C_Q = SSM_WIDTH
C_K = C_Q + ATT_WIDTH
C_V = C_K + ATT_WIDTH
C_G = C_V + ATT_WIDTH

VMEM_LIMIT_BYTES = 52 * 1024 * 1024
SUBLANES = 8
LANES = 128


def _cparams(sem):
    return pltpu.CompilerParams(dimension_semantics=sem, vmem_limit_bytes=VMEM_LIMIT_BYTES)


def _const_spec(shape):
    nd = len(shape)
    return pl.BlockSpec(shape, lambda *_: (0,) * nd)


def _layer_norm(x, g, b):
    mu = jnp.mean(x, axis=-1, keepdims=True)
    xc = x - mu
    var = jnp.mean(xc * xc, axis=-1, keepdims=True)
    return xc * lax.rsqrt(var + LN_EPS) * g + b


def _lambda_init(layer_idx):
    return 0.8 - 0.6 * math.exp(-0.3 * layer_idx)


def _diff_lambda(lamp, lam_init):
    a = jnp.sum(lamp[0:1, :] * lamp[1:2, :], axis=-1, keepdims=True)
    b = jnp.sum(lamp[2:3, :] * lamp[3:4, :], axis=-1, keepdims=True)
    return jnp.exp(a) - jnp.exp(b) + lam_init


def _store_heads(ref, val):
    n = val.shape[0]
    for h in range(N_HEADS):
        ref[pl.ds(h, n, stride=N_HEADS), :] = val[:, h * V_DIM:(h + 1) * V_DIM]


def _proj_rows_kernel(x_ref, w_ref, u_ref, q_ref, kf_ref, kb_ref, vf_ref, vb_ref):
    x = x_ref[...].astype(BF16)
    u_ref[...] = jnp.dot(x, w_ref[:, 0:C_Q], preferred_element_type=F32)
    q = jnp.dot(x, w_ref[:, C_Q:C_K], preferred_element_type=F32)
    q_ref[...] = (q * QK_SCALE).astype(BF16)
    k = jnp.dot(x, w_ref[:, C_K:C_V], preferred_element_type=F32)
    kb_ref[...] = k.astype(BF16)
    _store_heads(kf_ref, k)
    v = jnp.dot(x, w_ref[:, C_V:C_G], preferred_element_type=F32)
    vb_ref[...] = v.astype(BF16)
    _store_heads(vf_ref, v)


def _project_rows(x2d, w_bf16, tm):
    t = x2d.shape[0]
    row = lambda n: pl.BlockSpec((tm, n), lambda i: (i, 0))
    heads = pl.BlockSpec((tm * N_HEADS, V_DIM), lambda i: (i, 0))
    return pl.pallas_call(
        _proj_rows_kernel,
        grid=(t // tm,),
        in_specs=[row(D_MODEL), _const_spec((D_MODEL, C_G))],
        out_specs=[row(SSM_WIDTH), row(ATT_WIDTH), heads, row(ATT_WIDTH), heads, row(ATT_WIDTH)],
        out_shape=[
            jax.ShapeDtypeStruct((t, SSM_WIDTH), F32),
            jax.ShapeDtypeStruct((t, ATT_WIDTH), BF16),
            jax.ShapeDtypeStruct((t * N_HEADS, V_DIM), F32),
            jax.ShapeDtypeStruct((t, ATT_WIDTH), BF16),
            jax.ShapeDtypeStruct((t * N_HEADS, V_DIM), F32),
            jax.ShapeDtypeStruct((t, ATT_WIDTH), BF16),
        ],
        compiler_params=_cparams(("parallel",)),
        name="proj_rows",
    )(x2d, w_bf16)


def _proj_prompt_kernel(qblk, vblk, x_ref, w_ref, wt_ref, u_ref, kf_ref, kb_ref, vf_ref, qt_ref, vt_ref, kn_ref,
                        qn_ref):
    x = x_ref[...].astype(BF16)
    u_ref[...] = jnp.dot(x, w_ref[:, 0:C_Q], preferred_element_type=F32)
    k = jnp.dot(x, w_ref[:, C_Q:C_Q + ATT_WIDTH], preferred_element_type=F32)
    kb = k.astype(BF16)
    kb_ref[...] = kb
    kr = kb.astype(F32)
    member = (lax.broadcasted_iota(jnp.int32, (ATT_WIDTH, LANES), 0) >> 6) == lax.broadcasted_iota(
        jnp.int32, (ATT_WIDTH, LANES), 1)
    ksq = jnp.dot((kr * kr).astype(BF16), jnp.where(member, 1.0, 0.0).astype(BF16), preferred_element_type=F32)
    kn_ref[0] = jnp.broadcast_to(jnp.max(ksq, axis=0, keepdims=True), (SUBLANES, LANES))
    _store_heads(kf_ref, k)
    _store_heads(vf_ref, jnp.dot(x, w_ref[:, C_Q + ATT_WIDTH:], preferred_element_type=F32))
    nt = (((1,), (1,)), ((), ()))
    qt = lax.dot_general(wt_ref[0:ATT_WIDTH, :], x, nt, preferred_element_type=F32)
    qt = (qt * (QK_SCALE * LOG2E)).astype(BF16)
    vt = lax.dot_general(wt_ref[ATT_WIDTH:, :], x, nt, preferred_element_type=F32).astype(BF16)
    qr = qt.astype(F32)
    qsq = jnp.sum((qr * qr).reshape(2 * N_HEADS, HEAD_DIM, x.shape[0]), axis=1)
    for j in range(x.shape[0] // qblk):
        qt_ref[j] = qt[:, j * qblk:(j + 1) * qblk]
        qn_ref[j] = jnp.broadcast_to(jnp.max(qsq[:, j * qblk:(j + 1) * qblk], axis=1, keepdims=True),
                                     (2 * N_HEADS, LANES))
    for j in range(x.shape[0] // vblk):
        vt_ref[j] = vt[:, j * vblk:(j + 1) * vblk]


def _project_prompt(x2d, w_rows, w_cols, tm, qblk, vblk):
    t = x2d.shape[0]
    row = lambda n: pl.BlockSpec((tm, n), lambda i: (i, 0))
    heads = pl.BlockSpec((tm * N_HEADS, V_DIM), lambda i: (i, 0))
    tblk = lambda n: pl.BlockSpec((tm // n, ATT_WIDTH, n), lambda i: (i, 0, 0))
    return pl.pallas_call(
        functools.partial(_proj_prompt_kernel, qblk, vblk),
        grid=(t // tm,),
        in_specs=[row(D_MODEL), _const_spec(w_rows.shape), _const_spec(w_cols.shape)],
        out_specs=[row(SSM_WIDTH), heads, row(ATT_WIDTH), heads, tblk(qblk), tblk(vblk),
                   pl.BlockSpec((1, SUBLANES, LANES), lambda i: (i, 0, 0)),
                   pl.BlockSpec((tm // qblk, 2 * N_HEADS, LANES), lambda i: (i, 0, 0))],
        out_shape=[
            jax.ShapeDtypeStruct((t, SSM_WIDTH), F32),
            jax.ShapeDtypeStruct((t * N_HEADS, V_DIM), F32),
            jax.ShapeDtypeStruct((t, ATT_WIDTH), BF16),
            jax.ShapeDtypeStruct((t * N_HEADS, V_DIM), F32),
            jax.ShapeDtypeStruct((t // qblk, ATT_WIDTH, qblk), BF16),
            jax.ShapeDtypeStruct((t // vblk, ATT_WIDTH, vblk), BF16),
            jax.ShapeDtypeStruct((t // tm, SUBLANES, LANES), F32),
            jax.ShapeDtypeStruct((t // qblk, 2 * N_HEADS, LANES), F32),
        ],
        compiler_params=_cparams(("parallel",)),
        name="proj_prompt",
    )(x2d, w_rows, w_cols)


def _ssm_tables(lam_re, lam_im, log_dt, b_re, b_im, c_re, c_im):
    dt = jnp.exp(log_dt)[:, None]
    mag = jnp.exp(lam_re * dt)
    a_re, a_im = mag * jnp.cos(lam_im * dt), mag * jnp.sin(lam_im * dt)
    den = lam_re * lam_re + lam_im * lam_im
    zr, zi = a_re - 1.0, a_im
    fr = (zr * lam_re + zi * lam_im) / den
    fi = (zi * lam_re - zr * lam_im) / den
    bb_re = fr[..., None] * b_re - fi[..., None] * b_im
    bb_im = fr[..., None] * b_im + fi[..., None] * b_re
    eye = jnp.eye(SSM_GROUPS, dtype=F32)
    blk_b = lambda m: jnp.einsum('gpc,gh->gchp', m, eye).reshape(SSM_WIDTH, N_STATE).astype(BF16)
    blk_c = lambda m: jnp.einsum('gcp,gh->gphc', m, eye).reshape(N_STATE, SSM_WIDTH).astype(BF16)
    ar, ai = a_re.reshape(1, N_STATE), a_im.reshape(1, N_STATE)

    def cmul(xr, xi, yr, yi):
        return xr * yr - xi * yi, xr * yi + xi * yr

    rows = lax.broadcasted_iota(jnp.int32, (SUBLANES, 1), 0)
    tabs = []
    pr, pi = ar, ai
    for k in range(3):
        keep = rows >= (1 << k)
        tabs += [jnp.where(keep, pr, 0.0), jnp.where(keep, pi, 0.0)]
        pr, pi = cmul(pr, pi, pr, pi)
    cr, ci = [ar], [ai]
    for _ in range(SUBLANES - 1):
        nr, ni = cmul(cr[-1], ci[-1], ar, ai)
        cr.append(nr)
        ci.append(ni)
    tabs += [jnp.concatenate(cr, axis=0), jnp.concatenate(ci, axis=0)]
    return blk_b(bb_re), blk_b(bb_im), blk_c(c_re), blk_c(-c_im), jnp.stack(tabs)


SCAN_LANES = 512


def _scan_rows(sre, sim, mult_ref, r0, carry_re, carry_im):
    out_re, out_im = [], []
    for c0 in range(0, N_STATE, SCAN_LANES):
        ls = slice(c0, c0 + SCAN_LANES)
        xr = sre[pl.ds(r0, SUBLANES), ls]
        xi = sim[pl.ds(r0, SUBLANES), ls]
        for k in range(3):
            d = 1 << k
            yr = pltpu.roll(xr, d, 0)
            yi = pltpu.roll(xi, d, 0)
            mr = mult_ref[2 * k, :, ls]
            mi = mult_ref[2 * k + 1, :, ls]
            xr, xi = xr + (mr * yr - mi * yi), xi + (mr * yi + mi * yr)
        pr = mult_ref[6, :, ls]
        pi = mult_ref[7, :, ls]
        cr = carry_re[:, ls]
        ci = carry_im[:, ls]
        xr, xi = xr + (pr * cr - pi * ci), xi + (pr * ci + pi * cr)
        sre[pl.ds(r0, SUBLANES), ls] = xr
        sim[pl.ds(r0, SUBLANES), ls] = xi
        out_re.append(xr[SUBLANES - 1:SUBLANES, :])
        out_im.append(xi[SUBLANES - 1:SUBLANES, :])
    return jnp.concatenate(out_re, axis=1), jnp.concatenate(out_im, axis=1)


def _gelu_tanh(x):
    return 0.5 * x * (1.0 + jnp.tanh(math.sqrt(2.0 / math.pi) * (x + 0.044715 * (x * x * x))))


SSM_HALVES = 2


def _ssm_halves():
    cw, sw = SSM_WIDTH // SSM_HALVES, N_STATE // SSM_HALVES
    return [(slice(i * cw, (i + 1) * cw), slice(i * sw, (i + 1) * sw)) for i in range(SSM_HALVES)]


def _ssm_head(u_ref, bre_ref, bim_ref, sre, sim):
    ub = u_ref[...].astype(BF16)
    for cs, ss in _ssm_halves():
        sre[:, ss] = jnp.dot(ub[:, cs], bre_ref[cs, ss], preferred_element_type=F32)
        sim[:, ss] = jnp.dot(ub[:, cs], bim_ref[cs, ss], preferred_element_type=F32)


def _ssm_tail(u_ref, cre_ref, ncim_ref, d_ref, wglu_ref, bglu_ref, y_ref, sre, sim):
    y = jnp.concatenate(
        [jnp.dot(sre[:, ss].astype(BF16), cre_ref[ss, cs], preferred_element_type=F32)
         + jnp.dot(sim[:, ss].astype(BF16), ncim_ref[ss, cs], preferred_element_type=F32)
         for cs, ss in _ssm_halves()], axis=1) + d_ref[...] * u_ref[...]
    y = _gelu_tanh(y)
    z = jnp.dot(y.astype(BF16), wglu_ref[...], preferred_element_type=F32) + bglu_ref[...]
    y_ref[...] = (y * jax.nn.sigmoid(z)).astype(BF16)


def _ssm_seq_kernel(u_ref, bre_ref, bim_ref, cre_ref, ncim_ref, d_ref, wglu_ref, bglu_ref, mult_ref,
                    y_ref, ore_ref, oim_ref, sre, sim, car_re, car_im):
    @pl.when(pl.program_id(1) == 0)
    def _():
        car_re[...] = jnp.zeros_like(car_re)
        car_im[...] = jnp.zeros_like(car_im)

    _ssm_head(u_ref, bre_ref, bim_ref, sre, sim)
    n_blk = u_ref.shape[0] // SUBLANES

    def body(n, carry):
        r0 = pl.multiple_of(n * SUBLANES, SUBLANES)
        return _scan_rows(sre, sim, mult_ref, r0, carry[0], carry[1])

    cr, ci = lax.fori_loop(0, n_blk, body, (car_re[...], car_im[...]))
    car_re[...] = cr
    car_im[...] = ci
    ore_ref[0] = cr
    oim_ref[0] = ci
    _ssm_tail(u_ref, cre_ref, ncim_ref, d_ref, wglu_ref, bglu_ref, y_ref, sre, sim)


def _ssm_dec_kernel(u_ref, s0re_ref, s0im_ref, bre_ref, bim_ref, cre_ref, ncim_ref, d_ref, wglu_ref, bglu_ref,
                    mult_ref, y_ref, ore_ref, oim_ref, sre, sim):
    _ssm_head(u_ref, bre_ref, bim_ref, sre, sim)
    n_blk = u_ref.shape[0] // SUBLANES

    def body(n, _):
        r0 = pl.multiple_of(n * SUBLANES, SUBLANES)
        cr, ci = _scan_rows(sre, sim, mult_ref, r0, s0re_ref[pl.ds(n, 1), :], s0im_ref[pl.ds(n, 1), :])
        ore_ref[pl.ds(n, 1), :] = cr
        oim_ref[pl.ds(n, 1), :] = ci
        return 0

    lax.fori_loop(0, n_blk, body, 0)
    _ssm_tail(u_ref, cre_ref, ncim_ref, d_ref, wglu_ref, bglu_ref, y_ref, sre, sim)


def _ssm_weight_specs():
    return [_const_spec((SSM_WIDTH, N_STATE)), _const_spec((SSM_WIDTH, N_STATE)),
            _const_spec((N_STATE, SSM_WIDTH)), _const_spec((N_STATE, SSM_WIDTH)),
            _const_spec((1, SSM_WIDTH)), _const_spec((SSM_WIDTH, SSM_WIDTH)), _const_spec((1, SSM_WIDTH)),
            _const_spec((8, SUBLANES, N_STATE))]


def _ssm_prompt(u3, wts, lc):
    bsz, seq, _ = u3.shape
    return pl.pallas_call(
        _ssm_seq_kernel,
        grid=(bsz, seq // lc),
        in_specs=[pl.BlockSpec((None, lc, SSM_WIDTH), lambda b, c: (b, c, 0))] + _ssm_weight_specs(),
        out_specs=[pl.BlockSpec((None, lc, SSM_WIDTH), lambda b, c: (b, c, 0)),
                   pl.BlockSpec((1, 1, N_STATE), lambda b, c: (b, 0, 0)),
                   pl.BlockSpec((1, 1, N_STATE), lambda b, c: (b, 0, 0))],
        out_shape=[jax.ShapeDtypeStruct((bsz, seq, SSM_WIDTH), BF16),
                   jax.ShapeDtypeStruct((bsz, 1, N_STATE), F32),
                   jax.ShapeDtypeStruct((bsz, 1, N_STATE), F32)],
        scratch_shapes=[pltpu.VMEM((lc, N_STATE), F32), pltpu.VMEM((lc, N_STATE), F32),
                        pltpu.VMEM((1, N_STATE), F32), pltpu.VMEM((1, N_STATE), F32)],
        compiler_params=_cparams(("parallel", "arbitrary")),
        name="ssm_prompt",
    )(u3, *wts)


def _ssm_sample(u2, s0_re, s0_im, wts, lc):
    t = u2.shape[0]
    nseq = lc // SUBLANES
    return pl.pallas_call(
        _ssm_dec_kernel,
        grid=(t // lc,),
        in_specs=[pl.BlockSpec((lc, SSM_WIDTH), lambda c: (c, 0)),
                  pl.BlockSpec((nseq, N_STATE), lambda c: (c, 0)),
                  pl.BlockSpec((nseq, N_STATE), lambda c: (c, 0))] + _ssm_weight_specs(),
        out_specs=[pl.BlockSpec((lc, SSM_WIDTH), lambda c: (c, 0)),
                   pl.BlockSpec((nseq, N_STATE), lambda c: (c, 0)),
                   pl.BlockSpec((nseq, N_STATE), lambda c: (c, 0))],
        out_shape=[jax.ShapeDtypeStruct((t, SSM_WIDTH), BF16),
                   jax.ShapeDtypeStruct((t // SUBLANES, N_STATE), F32),
                   jax.ShapeDtypeStruct((t // SUBLANES, N_STATE), F32)],
        scratch_shapes=[pltpu.VMEM((lc, N_STATE), F32), pltpu.VMEM((lc, N_STATE), F32)],
        compiler_params=_cparams(("parallel",)),
        name="ssm_sample",
    )(u2, s0_re, s0_im, *wts)


def _head_rms(o, g, lam_init):
    ms = jnp.mean(o * o, axis=-1, keepdims=True)
    return o * lax.rsqrt(ms + LN_EPS) * g * (1.0 - lam_init)


N_BIAS_TERMS = 3
ATTN_BQ = 256
ATTN_BK = 256
ATTN_HEADS = 4
ATTN_UNROLL = 4
UNDERFLOW_BITS = 160.0
NORM_SLACK = 1.05
ONES_ROWS = 16


def _attn_kernel(lam_init, bq, bk, nh, slopes_ref, lamp_ref, g_ref, qt_ref, k_ref, vt_ref, kn_ref, qn_ref, o_ref,
                 acc_ref, m_ref, a_ref, u_ref, p_ref):
    hg = pl.program_id(1)
    qi = pl.program_id(2)
    two = 2 * bq
    n_diag = bq // bk
    heads = range(nh)
    hs = lambda hh: slice(hh * V_DIM, (hh + 1) * V_DIM)

    slope2, w, bias_cols = [], [], []
    r = lax.broadcasted_iota(jnp.int32, (V_DIM, bq), 0)
    ones_rows = lax.broadcasted_iota(jnp.int32, (V_DIM, two), 0) < N_BIAS_TERMS
    c = lax.broadcasted_iota(jnp.int32, (bk, V_DIM), 0).astype(F32)
    lane = lax.broadcasted_iota(jnp.int32, (bk, V_DIM), 1)
    for hh in heads:
        slope2.append(slopes_ref[hg * nh + hh] * LOG2E)
        qt = qt_ref[0, hs(hh), :]
        zero = jnp.zeros_like(qt)
        w_top = jnp.concatenate([jnp.where(r < HEAD_DIM, qt, zero), jnp.where(r >= HEAD_DIM, qt, zero)], axis=1)
        w.append(jnp.concatenate([w_top, jnp.where(ones_rows, 1.0, 0.0).astype(BF16)], axis=0))
        rem = slope2[hh] * c
        cols = jnp.zeros((bk, V_DIM), F32)
        for t in range(N_BIAS_TERMS):
            term = rem.astype(BF16).astype(F32)
            cols = jnp.where(lane == t, term, cols)
            rem = rem - term
        bias_cols.append(cols.astype(BF16))

    ksq_all = jnp.max(jnp.max(kn_ref[...], axis=0), axis=0, keepdims=True)
    klane = lax.broadcasted_iota(jnp.int32, (1, LANES), 1) >> 1
    first_f = []
    for hh in heads:
        kn = jnp.sqrt(jnp.max(jnp.where(klane == hg * nh + hh, ksq_all, 0.0), axis=1, keepdims=True))
        qsq = qn_ref[0, pl.ds(2 * (hg * nh + hh), 2), :]
        qn = jnp.sqrt(jnp.max(qsq, axis=0, keepdims=True)[:, 0:1])
        reach = (NORM_SLACK * 2.0 * qn * kn + UNDERFLOW_BITS) / slope2[hh] + bk
        first_f.append(jnp.floor((jnp.full((1, 1), qi * bq, jnp.int32).astype(F32) - reach) * (1.0 / bk)) + 1.0)
    n_full = qi * n_diag

    def start_block(hds):
        f = functools.reduce(jnp.minimum, [first_f[hh] for hh in hds])
        return jnp.minimum(jnp.clip(f, 0.0, 1e9).astype(jnp.int32)[0, 0] & -2, n_full & -2)

    near_heads = list(heads)[:max(nh // 2, 1)]
    far_heads = list(heads)[len(near_heads):]
    first_near = start_block(near_heads)
    first_all = jnp.minimum(start_block(list(heads)), first_near)

    ones_rows16 = jnp.ones((ONES_ROWS, bk), BF16)

    m_ref[...] = jnp.full(m_ref.shape, NEG_INF, F32)
    a_ref[...] = jnp.ones(a_ref.shape, F32)
    acc_ref[...] = jnp.zeros(acc_ref.shape, F32)
    for hh in heads:
        p_ref[2 * hh + 1] = jnp.zeros((bk, two), BF16)

    def scores(hh, kj, slot):
        ks = pl.multiple_of(kj * bk, bk)
        kaug = jnp.concatenate([k_ref[pl.ds(ks, bk), hs(hh)], bias_cols[hh]], axis=1)
        u_ref[2 * hh + slot] = jnp.dot(kaug, w[hh], preferred_element_type=F32)

    def softmax(hh, kj, slot, diag):
        u = u_ref[2 * hh + slot]
        if diag is not None:
            keep = (lax.broadcasted_iota(jnp.int32, (bk, two), 1) & (bq - 1)) >= (
                lax.broadcasted_iota(jnp.int32, (bk, two), 0) + diag * bk)
            u = jnp.where(keep, u, NEG_INF)
        off = jnp.full((1, 1), kj * bk - qi * bq, jnp.int32).astype(F32) * slope2[hh]
        m_old = m_ref[hh]
        m_new = jnp.maximum(m_old, jnp.max(u, axis=0, keepdims=True) + off)
        a_ref[hh] = jnp.exp2(m_old - m_new)
        m_ref[hh] = m_new
        p_ref[2 * hh + slot] = jnp.exp2(u + (off - m_new)).astype(BF16)

    def values(hh, kj, slot):
        vaug = jnp.concatenate([vt_ref[jnp.maximum(kj, 0), hs(hh), :], ones_rows16], axis=0)
        acc_ref[hh] = a_ref[hh] * acc_ref[hh] + jnp.dot(vaug, p_ref[2 * hh + slot], preferred_element_type=F32)

    def run(kj, slot, hds):
        for hh in hds:
            values(hh, kj - 1, 1 - slot)
            softmax(hh, kj, slot, None)
            scores(hh, kj + 1, 1 - slot)

    def span(start, n_run, hds, odd_possible):
        def body(i, carry):
            for d in range(ATTN_UNROLL):
                run(start + i * ATTN_UNROLL + d, d & 1, hds)
            return carry

        n_main = n_run // ATTN_UNROLL
        lax.fori_loop(0, n_main, body, 0)
        base = start + n_main * ATTN_UNROLL
        arm = ATTN_UNROLL // 2
        while arm >= 1:
            if arm > 1 or odd_possible:
                @pl.when((n_run & arm) != 0)
                def _(base=base, arm=arm):
                    for d in range(arm):
                        run(base + d, d & 1, hds)
                base = base + (n_run & arm)
            arm //= 2

    for hh in far_heads:
        scores(hh, first_all, 0)
    if far_heads:
        span(first_all, first_near - first_all, far_heads, False)
    for hh in near_heads:
        scores(hh, first_near, 0)
    span(first_near, n_full - first_near, list(heads), n_diag % 2 != 0)
    for d in range(n_diag):
        slot = d & 1 if n_diag % 2 == 0 else (n_full + d) & 1
        for hh in heads:
            values(hh, n_full + d - 1, 1 - slot)
            softmax(hh, n_full + d, slot, d)
            if d + 1 < n_diag:
                scores(hh, n_full + d + 1, 1 - slot)
    lam = _diff_lambda(lamp_ref[...], lam_init)
    for hh in heads:
        values(hh, n_full + n_diag - 1, slot)
        acc = acc_ref[hh]
        inv = 1.0 / acc[V_DIM:V_DIM + 1, :]
        acc = acc[:V_DIM, :]
        ot = acc[:, :bq] * inv[:, :bq] - lam * (acc[:, bq:] * inv[:, bq:])
        ms = jnp.mean(ot * ot, axis=0, keepdims=True)
        ot = ot * lax.rsqrt(ms + LN_EPS) * (g_ref[...] * (1.0 - lam_init))
        o_ref[0, :, hs(hh)] = ot.T.astype(o_ref.dtype)


def _attn_prompt(qt, k, vt, kn, qn, slopes, lamp, g_col, lam_init, bsz, seq, bq, bk, nh):
    nq, nk = seq // bq, seq // bk
    hw = nh * V_DIM
    kn_tiles = kn.shape[0] // bsz
    return pl.pallas_call(
        functools.partial(_attn_kernel, lam_init, bq, bk, nh),
        grid=(bsz, N_HEADS // nh, nq),
        in_specs=[pl.BlockSpec(memory_space=pltpu.SMEM),
                  _const_spec((4, HEAD_DIM)), _const_spec((V_DIM, 1)),
                  pl.BlockSpec((1, hw, bq), lambda b, h, i: (b * nq + i, h, 0)),
                  pl.BlockSpec((seq, hw), lambda b, h, i: (b, h)),
                  pl.BlockSpec((nk, hw, bk), lambda b, h, i: (b, h, 0)),
                  pl.BlockSpec((kn_tiles, SUBLANES, LANES), lambda b, h, i: (b, 0, 0)),
                  pl.BlockSpec((1, 2 * N_HEADS, LANES), lambda b, h, i: (b * nq + i, 0, 0))],
        out_specs=pl.BlockSpec((1, bq, hw), lambda b, h, i: (b, i, h)),
        out_shape=jax.ShapeDtypeStruct((bsz, seq, ATT_WIDTH), BF16),
        scratch_shapes=[pltpu.VMEM((nh, V_DIM + ONES_ROWS, 2 * bq), F32),
                        pltpu.VMEM((nh, 1, 2 * bq), F32), pltpu.VMEM((nh, 1, 2 * bq), F32),
                        pltpu.VMEM((2 * nh, bk, 2 * bq), F32), pltpu.VMEM((2 * nh, bk, 2 * bq), BF16)],
        compiler_params=_cparams(("parallel", "parallel", "arbitrary")),
        name="attn_prompt",
    )(slopes, lamp, g_col, qt, k, vt, kn, qn)


def _page_matrix(ref):
    return jnp.concatenate([ref[0, pl.ds(h, PAGE_SIZE, stride=N_HEADS), :].astype(BF16) for h in range(N_HEADS)],
                           axis=1)


def _sattn_kernel(lam_init, n_pages, pt_ref, lamp_ref, g_ref, q_ref, kn_ref, vn_ref, *rest):
    del pt_ref
    k_refs = rest[:n_pages]
    v_refs = rest[n_pages:2 * n_pages]
    o_ref = rest[2 * n_pages]
    acc_scr = rest[2 * n_pages + 1]
    t_new = q_ref.shape[1]
    n_past = n_pages * PAGE_SIZE
    nl = 2 * N_HEADS * t_new
    nt = (((1,), (1,)), ((), ()))
    tn = (((0,), (0,)), ((), ()))

    c = lax.broadcasted_iota(jnp.int32, (1, nl), 1)
    c_q = c & (t_new - 1)
    c_h = (c >> 3) & (N_HEADS - 1)
    c_m = c >> 6
    slope = jnp.exp2(-(c_h + 1).astype(F32))
    qpos = (n_past + c_q).astype(F32)

    r = lax.broadcasted_iota(jnp.int32, (nl, t_new), 0)
    sel = (r & (t_new - 1)) == lax.broadcasted_iota(jnp.int32, (nl, t_new), 1)
    rep = jnp.dot(sel.astype(BF16), q_ref[0], preferred_element_type=F32)
    rr = lax.broadcasted_iota(jnp.int32, (nl, ATT_WIDTH), 0)
    jj = lax.broadcasted_iota(jnp.int32, (nl, ATT_WIDTH), 1)
    same_head = ((rr >> 3) & (N_HEADS - 1)) == (jj >> 7)
    keep = same_head & ((rr >> 6) == ((jj >> 6) & 1))
    qbd = jnp.where(keep, rep, 0.0).astype(BF16)

    def rows_scaled(v):
        return jnp.concatenate([jnp.broadcast_to(v, (nl, nl)).T] * (ATT_WIDTH // nl), axis=1)

    dist = c_q - lax.broadcasted_iota(jnp.int32, (t_new, 1), 0)
    sn = lax.dot_general(kn_ref[0].astype(BF16), qbd, nt, preferred_element_type=F32)
    sn = jnp.where(dist >= 0, sn - slope * dist.astype(F32), NEG_INF)
    m = jnp.max(sn, axis=0, keepdims=True)
    pn = jnp.exp(sn - m)
    l = jnp.sum(pn, axis=0, keepdims=True)
    acc_scr[...] = lax.dot_general(pn.astype(BF16), vn_ref[0].astype(BF16), tn, preferred_element_type=F32)

    row = lax.broadcasted_iota(jnp.int32, (PAGE_SIZE, 1), 0).astype(F32)
    for i in range(n_pages):
        s = lax.dot_general(_page_matrix(k_refs[i]), qbd, nt, preferred_element_type=F32)
        s = s - slope * (qpos - (row + float(i * PAGE_SIZE)))
        m_new = jnp.maximum(m, jnp.max(s, axis=0, keepdims=True))
        alpha = jnp.exp(m - m_new)
        p = jnp.exp(s - m_new)
        l = alpha * l + jnp.sum(p, axis=0, keepdims=True)
        m = m_new
        pv = lax.dot_general(p.astype(BF16), _page_matrix(v_refs[i]), tn, preferred_element_type=F32)
        acc_scr[...] = acc_scr[...] * rows_scaled(alpha) + pv

    lam = _diff_lambda(lamp_ref[...], lam_init)
    coef = jnp.where(c_m == 0, 1.0, -lam) / l
    acc = jnp.where(same_head, acc_scr[...] * rows_scaled(coef), 0.0)
    o = jnp.sum(acc.reshape(nl // t_new, t_new, ATT_WIDTH), axis=0)
    for hh in range(N_HEADS):
        ls = slice(hh * V_DIM, (hh + 1) * V_DIM)
        o_ref[0, :, ls] = _head_rms(o[:, ls], g_ref[...], lam_init).astype(o_ref.dtype)


def _attn_sample(q, kn, vn, cache_k, cache_v, page_table, lamp, g, lam_init):
    nseq, t_new, _ = q.shape
    n_pages = page_table.shape[1]
    tok = pl.BlockSpec((1, t_new, ATT_WIDTH), lambda s, pt: (s, 0, 0))
    page = lambda i: pl.BlockSpec((1, PAGE_SIZE * N_HEADS, V_DIM), lambda s, pt: (pt[s, i], 0, 0))
    grid_spec = pltpu.PrefetchScalarGridSpec(
        num_scalar_prefetch=1,
        grid=(nseq,),
        in_specs=[pl.BlockSpec((4, HEAD_DIM), lambda s, pt: (0, 0)), pl.BlockSpec((1, V_DIM), lambda s, pt: (0, 0)),
                  tok, tok, tok]
                 + [page(i) for i in range(n_pages)] + [page(i) for i in range(n_pages)],
        out_specs=tok,
        scratch_shapes=[pltpu.VMEM((2 * N_HEADS * t_new, ATT_WIDTH), F32)],
    )
    return pl.pallas_call(
        functools.partial(_sattn_kernel, lam_init, n_pages),
        grid_spec=grid_spec,
        out_shape=jax.ShapeDtypeStruct((nseq, t_new, ATT_WIDTH), BF16),
        compiler_params=_cparams(("arbitrary",)),
        name="attn_sample",
    )(page_table, lamp, g, q, kn, vn, *([cache_k] * n_pages), *([cache_v] * n_pages))


MERGE_ROWS = 512


def _merge_kernel(x_ref, ssm_ref, att_ref, wg_ref, bg_ref, pssm_ref, patt_ref, wo_ref, g_ref, b_ref, o_ref):
    x = x_ref[...]
    gates = jax.nn.sigmoid(jnp.dot(x.astype(BF16), wg_ref[...], preferred_element_type=F32) + bg_ref[...])
    merged = (gates[:, :D_MODEL] * jnp.dot(ssm_ref[...], pssm_ref[...], preferred_element_type=F32)
              + gates[:, D_MODEL:] * jnp.dot(att_ref[...], patt_ref[...], preferred_element_type=F32))
    y = DN_ALPHA * x + jnp.dot(merged.astype(BF16), wo_ref[...], preferred_element_type=F32)
    o_ref[...] = _layer_norm(y, g_ref[...], b_ref[...])


def _merge(x2d, ssm_y, att, wg, bg, pssm, patt, wo, g, b, tm):
    t = x2d.shape[0]
    row = lambda n: pl.BlockSpec((tm, n), lambda i: (i, 0))
    return pl.pallas_call(
        _merge_kernel,
        grid=(t // tm,),
        in_specs=[row(D_MODEL), row(SSM_WIDTH), row(ATT_WIDTH),
                  _const_spec((D_MODEL, 2 * D_MODEL)), _const_spec((1, 2 * D_MODEL)),
                  _const_spec((SSM_WIDTH, D_MODEL)), _const_spec((ATT_WIDTH, D_MODEL)),
                  _const_spec((D_MODEL, D_MODEL)), _const_spec((1, D_MODEL)), _const_spec((1, D_MODEL))],
        out_specs=row(D_MODEL),
        out_shape=jax.ShapeDtypeStruct((t, D_MODEL), F32),
        compiler_params=_cparams(("parallel",)),
        name="merge",
    )(x2d, ssm_y, att, wg, bg, pssm, patt, wo, g, b)


def _route(x_bf16, wrg, brg, wre, bre):
    gl = jnp.dot(x_bf16, wrg, preferred_element_type=F32) + brg
    el = jnp.dot(x_bf16, wre, preferred_element_type=F32) + bre
    gi = lax.broadcasted_iota(jnp.int32, gl.shape, 1).astype(F32)
    gmax = jnp.max(gl, axis=-1, keepdims=True)
    g_sel = jnp.min(jnp.where(gl == gmax, gi, float(N_EXPERT_GROUPS)), axis=-1, keepdims=True)
    p_g = 1.0 / jnp.sum(jnp.exp(gl - gmax), axis=-1, keepdims=True)
    ei_int = lax.broadcasted_iota(jnp.int32, el.shape, 1)
    ei = ei_int.astype(F32)
    el = jnp.where((ei_int >> 3).astype(F32) == g_sel, el, -jnp.inf)
    v1 = jnp.max(el, axis=-1, keepdims=True)
    i1 = jnp.min(jnp.where(el == v1, ei, float(N_EXPERTS)), axis=-1, keepdims=True)
    el2 = jnp.where(ei == i1, -jnp.inf, el)
    v2 = jnp.max(el2, axis=-1, keepdims=True)
    i2 = jnp.min(jnp.where(el2 == v2, ei, float(N_EXPERTS)), axis=-1, keepdims=True)
    e2 = jnp.exp(v2 - v1)
    w1 = p_g / (1.0 + e2)
    w2 = p_g * e2 / (1.0 + e2)
    lane = lax.broadcasted_iota(jnp.int32, (gl.shape[0], LANES), 1).astype(F32)
    first = EXPERTS_PER_GROUP * g_sel
    comb = jnp.where(lane == i1 - first, w1, 0.0) + jnp.where(lane == i2 - first, w2, 0.0)
    return jnp.where(lane == g_sel, 1.0, 0.0), comb


MOE_ROWS = 128
MOE_EXPERTS_PER_STEP = 4


def _moe_kernel(x_ref, wrg_ref, brg_ref, wre_ref, bre_ref, wg_ref, wu_ref, wd_ref, g_ref, b_ref, o_ref,
                xs_scr, comb_scr, acc_scr, unsort_scr, seg_ref):
    e = pl.program_id(1)
    tm = x_ref.shape[0]
    srows = xs_scr.shape[0]
    lane = lax.broadcasted_iota(jnp.int32, (1, LANES), 1)

    @pl.when(e == 0)
    def _():
        xb = x_ref[...].astype(BF16)
        group_1h, comb = _route(xb, wrg_ref[...], brg_ref[...], wre_ref[...], bre_ref[...])
        tri = lax.broadcasted_iota(jnp.int32, (tm, tm), 0) >= lax.broadcasted_iota(jnp.int32, (tm, tm), 1)
        count = jnp.dot(jnp.where(tri, 1.0, 0.0).astype(BF16), group_1h.astype(BF16), preferred_element_type=F32)
        total = count[tm - 1:tm, :]
        padded = jnp.floor((total + (MOE_ROWS - 1)) * (1.0 / MOE_ROWS)) * MOE_ROWS
        start = jnp.zeros((1, LANES), F32)
        for gidx in range(N_EXPERT_GROUPS):
            size = jnp.sum(jnp.where(lane == gidx, padded, 0.0), axis=-1, keepdims=True)
            seg_ref[0, gidx] = jnp.sum(jnp.where(lane == gidx, start, 0.0)).astype(jnp.int32) // MOE_ROWS
            seg_ref[1, gidx] = jnp.sum(size).astype(jnp.int32) // MOE_ROWS
            start = start + jnp.where(lane > gidx, size, 0.0)
        pos = jnp.sum(group_1h * (start + count - 1.0), axis=-1, keepdims=True)
        unsort = jnp.where(lax.broadcasted_iota(jnp.int32, (tm, srows), 1).astype(F32) == pos, 1.0, 0.0)
        unsort_scr[...] = unsort.astype(BF16)
        pos_row = jnp.broadcast_to(pos, (tm, LANES)).T[0:1, :]
        sort = jnp.where(lax.broadcasted_iota(jnp.int32, (srows, tm), 0).astype(F32) == pos_row, 1.0, 0.0)
        sort = sort.astype(BF16)
        xs_scr[...] = jnp.dot(sort, xb, preferred_element_type=F32).astype(BF16)
        comb_hi = comb.astype(BF16)
        comb_lo = (comb - comb_hi.astype(F32)).astype(BF16)
        comb_s = jnp.dot(sort, jnp.concatenate([comb_hi, comb_lo], axis=1), preferred_element_type=F32)
        comb_scr[...] = comb_s[:, :LANES] + comb_s[:, LANES:]
        acc_scr[...] = jnp.zeros_like(acc_scr)

    first_expert = e * MOE_EXPERTS_PER_STEP
    grp = first_expert >> 3
    in_grp = first_expert & (EXPERTS_PER_GROUP - 1)
    first_blk = seg_ref[0, grp]

    def block(i):
        r0 = pl.multiple_of((first_blk + i) * MOE_ROWS, MOE_ROWS)
        xs = xs_scr[pl.ds(r0, MOE_ROWS), :]
        comb = comb_scr[pl.ds(r0, MOE_ROWS), :]
        out = None
        for k in range(MOE_EXPERTS_PER_STEP):
            hg = jnp.dot(xs, wg_ref[k], preferred_element_type=F32)
            hu = jnp.dot(xs, wu_ref[k], preferred_element_type=F32)
            ce = jnp.sum(jnp.where(lane == in_grp + k, comb, 0.0), axis=-1, keepdims=True)
            hid = (hg * jax.nn.sigmoid(hg)) * hu * ce
            part = jnp.dot(hid.astype(BF16), wd_ref[k], preferred_element_type=F32)
            out = part if out is None else out + part
        acc_scr[pl.ds(r0, MOE_ROWS), :] += out

    n_blk = seg_ref[1, grp]

    def pair(i, carry):
        block(2 * i)
        block(2 * i + 1)
        return carry

    lax.fori_loop(0, n_blk >> 1, pair, 0)

    @pl.when((n_blk & 1) != 0)
    def _():
        block(n_blk - 1)

    @pl.when(e == N_EXPERTS // MOE_EXPERTS_PER_STEP - 1)
    def _():
        ffn = jnp.dot(unsort_scr[...], acc_scr[...].astype(BF16), preferred_element_type=F32)
        o_ref[...] = _layer_norm(DN_ALPHA * x_ref[...] + ffn, g_ref[...], b_ref[...])


def _moe(x1, wrg, brg, wre, bre, wg, wu, wd, g, b, tm):
    t = x1.shape[0]
    srows = tm + N_EXPERT_GROUPS * MOE_ROWS
    row = pl.BlockSpec((tm, D_MODEL), lambda i, e: (i, 0))
    return pl.pallas_call(
        _moe_kernel,
        grid=(t // tm, N_EXPERTS // MOE_EXPERTS_PER_STEP),
        in_specs=[row,
                  _const_spec((D_MODEL, N_EXPERT_GROUPS)), _const_spec((1, N_EXPERT_GROUPS)),
                  _const_spec((D_MODEL, N_EXPERTS)), _const_spec((1, N_EXPERTS)),
                  pl.BlockSpec((MOE_EXPERTS_PER_STEP, D_MODEL, EXPERT_FF), lambda i, e: (e, 0, 0)),
                  pl.BlockSpec((MOE_EXPERTS_PER_STEP, D_MODEL, EXPERT_FF), lambda i, e: (e, 0, 0)),
                  pl.BlockSpec((MOE_EXPERTS_PER_STEP, EXPERT_FF, D_MODEL), lambda i, e: (e, 0, 0)),
                  _const_spec((1, D_MODEL)), _const_spec((1, D_MODEL))],
        out_specs=row,
        out_shape=jax.ShapeDtypeStruct((t, D_MODEL), F32),
        scratch_shapes=[pltpu.VMEM((srows, D_MODEL), BF16), pltpu.VMEM((srows, LANES), F32),
                        pltpu.VMEM((srows, D_MODEL), F32), pltpu.VMEM((tm, srows), BF16),
                        pltpu.SMEM((2, N_EXPERT_GROUPS), jnp.int32)],
        compiler_params=_cparams(("parallel", "arbitrary")),
        name="moe",
    )(x1, wrg, brg, wre, bre, wg, wu, wd, g, b)


class _Tiles(NamedTuple):
    proj: int
    ssm: int
    attn_q: int
    attn_k: int
    merge: int
    moe: int


def _tiles(seq, tokens):
    return _Tiles(proj=min(512, seq), ssm=min(512, seq), attn_q=min(ATTN_BQ, seq), attn_k=min(ATTN_BK, seq),
                  merge=min(MERGE_ROWS, tokens), moe=min(1024, tokens))


def kernel(x_prompt, x_sample, cache_k, cache_v, state_ssm_re, state_ssm_im, page_table, w_in, b_gate, ssm_lambda_re, ssm_lambda_im, ssm_log_dt, ssm_b_re, ssm_b_im, ssm_c_re, ssm_c_im, ssm_d, w_glu, b_glu, lambda_q1, lambda_k1, lambda_q2, lambda_k2, subln_g, p_ssm, p_att, w_o, ln1_g, ln1_b, w_router_group, b_router_group, w_router_expert, b_router_expert, w_exp_gate, w_exp_up, w_exp_down, ln2_g, ln2_b):
    bsz, seq, _ = x_prompt.shape
    nseq, t_new, _ = x_sample.shape
    assert w_in.shape[0] == DEPTH == 1 and t_new == SUBLANES
    l = 0
    lam_init = _lambda_init(l)
    row = lambda a: a[l].reshape(1, -1)

    w_qkv = w_in[l][:, :C_G].astype(BF16)
    w_gate = w_in[l][:, C_G:].astype(BF16)
    w_ukv = jnp.concatenate([w_qkv[:, :C_Q], w_qkv[:, C_K:]], axis=1)
    w_qv_t = jnp.concatenate([w_qkv[:, C_Q:C_K], w_qkv[:, C_V:]], axis=1).T
    ssm_tabs = _ssm_tables(ssm_lambda_re[l], ssm_lambda_im[l], ssm_log_dt[l], ssm_b_re[l], ssm_b_im[l],
                           ssm_c_re[l], ssm_c_im[l])
    bre, bim, cre, ncim, mult = ssm_tabs
    ssm_wts = (bre, bim, cre, ncim, row(ssm_d), w_glu[l].astype(BF16), row(b_glu), mult)
    lamp = jnp.stack([lambda_q1[l], lambda_k1[l], lambda_q2[l], lambda_k2[l]])
    g_sub = row(subln_g)
    slopes = jnp.exp2(-8.0 * (jnp.arange(N_HEADS, dtype=F32) + 1.0) / N_HEADS)
    merge_wts = (w_gate, row(b_gate), p_ssm[l].astype(BF16), p_att[l].astype(BF16), w_o[l].astype(BF16),
                 row(ln1_g), row(ln1_b))
    moe_wts = (w_router_group[l].astype(BF16), row(b_router_group), w_router_expert[l].astype(BF16),
               row(b_router_expert), w_exp_gate[l].astype(BF16), w_exp_up[l].astype(BF16),
               w_exp_down[l].astype(BF16), row(ln2_g), row(ln2_b))

    tp = bsz * seq
    xp = x_prompt.reshape(tp, D_MODEL)
    tl = _tiles(seq, tp)
    u, kf, kb, vf, qt, vt, kn, qn = _project_prompt(xp, w_ukv, w_qv_t, tl.proj, tl.attn_q, tl.attn_k)
    ssm_y, srp, sip = _ssm_prompt(u.reshape(bsz, seq, SSM_WIDTH), ssm_wts, tl.ssm)
    att = _attn_prompt(qt, kb, vt, kn, qn, slopes, lamp, subln_g[l].reshape(V_DIM, 1), lam_init, bsz, seq,
                       tl.attn_q, tl.attn_k, ATTN_HEADS)
    x1 = _merge(xp, ssm_y.reshape(tp, SSM_WIDTH), att.reshape(tp, ATT_WIDTH), *merge_wts, tl.merge)
    y_prompt = _moe(x1, *moe_wts, tl.moe).reshape(bsz, seq, D_MODEL)
    k_prompt = kf.reshape(1, bsz, seq, N_HEADS, V_DIM)
    v_prompt = vf.reshape(1, bsz, seq, N_HEADS, V_DIM)
    srp = srp.reshape(1, bsz, SSM_GROUPS, SSM_STATE)
    sip = sip.reshape(1, bsz, SSM_GROUPS, SSM_STATE)

    ts = nseq * t_new
    xs = x_sample.reshape(ts, D_MODEL)
    tl = _tiles(ts, ts)
    u, q, kf, kb, vf, vb = _project_rows(xs, w_qkv, tl.proj)
    ssm_y, srs, sis = _ssm_sample(u, state_ssm_re[l].reshape(nseq, N_STATE), state_ssm_im[l].reshape(nseq, N_STATE),
                                  ssm_wts, tl.ssm)
    att = _attn_sample(q.reshape(nseq, t_new, ATT_WIDTH), kb.reshape(nseq, t_new, ATT_WIDTH),
                       vb.reshape(nseq, t_new, ATT_WIDTH),
                       cache_k[l].reshape(-1, PAGE_SIZE * N_HEADS, V_DIM),
                       cache_v[l].reshape(-1, PAGE_SIZE * N_HEADS, V_DIM), page_table, lamp, g_sub, lam_init)
    x1 = _merge(xs, ssm_y, att.reshape(ts, ATT_WIDTH), *merge_wts, tl.merge)
    y_sample = _moe(x1, *moe_wts, tl.moe).reshape(nseq, t_new, D_MODEL)
    k_sample = kf.reshape(1, nseq, t_new, N_HEADS, V_DIM)
    v_sample = vf.reshape(1, nseq, t_new, N_HEADS, V_DIM)
    srs = srs.reshape(1, nseq, SSM_GROUPS, SSM_STATE)
    sis = sis.reshape(1, nseq, SSM_GROUPS, SSM_STATE)

    return (y_prompt, y_sample, k_prompt, v_prompt, srp, sip, k_sample, v_sample, srs, sis)
```
